```python
import math
import jax, jax.numpy as jnp
from jax import lax
import numpy as np

D_MODEL = 1024
BATCH = 2
SEQ = 8192
DEPTH = 4
DEC_BATCH = 128
DEC_SEQ = 4
PAST_LEN = 2048
PAGE_SIZE = 128

N_A_LAYERS = DEPTH // 2
N_B_LAYERS = DEPTH - N_A_LAYERS
POOL_WINDOWS = (2, 4, 8, 16)
N_POOL_GROUPS = len(POOL_WINDOWS)
POOL_GROUP_DIM = D_MODEL // N_POOL_GROUPS
POOL_BUF = max(POOL_WINDOWS) - 1
N_HEADS = 16
HEAD_DIM = D_MODEL // N_HEADS
D_FF = 4 * D_MODEL
Q_BLOCK = 128
EPS = 1e-6
FORGET_BIAS = 4.0

kernel_name = "yoco_pool_fox_decoder_step"


def rmsnorm(x, g):
    xf = x.astype(jnp.float32)
    y = xf * lax.rsqrt(jnp.mean(xf * xf, axis=-1, keepdims=True) + EPS)
    return (y * g.astype(jnp.float32)).astype(x.dtype)


def sq_relu_mlp(h, g, w_up, w_down):
    a = jax.nn.relu(rmsnorm(h, g) @ w_up)
    return (a * a) @ w_down


def pool_mixer(xn_ext, pos, w_grp, scale):
    B = xn_ext.shape[0]
    T = pos.shape[0]
    xf = xn_ext.astype(jnp.float32)
    cs = jnp.concatenate([jnp.zeros_like(xf[:, :1]), jnp.cumsum(xf, axis=1)], axis=1)
    end = cs[:, POOL_BUF + 1:]
    x_new = xf[:, POOL_BUF:]
    outs = []
    for gi, w in enumerate(POOL_WINDOWS):
        sl = slice(gi * POOL_GROUP_DIM, (gi + 1) * POOL_GROUP_DIM)
        start = cs[:, POOL_BUF + 1 - w: POOL_BUF + 1 - w + T, sl]
        cnt = jnp.minimum(pos + 1, w).astype(jnp.float32)[None, :, None]
        outs.append((end[..., sl] - start) / cnt - x_new[..., sl])
    pooled = jnp.stack(outs, axis=2)
    mixed = jnp.einsum('btgc,gcd->btgd', pooled, w_grp.astype(jnp.float32)).reshape(B, T, D_MODEL)
    return (mixed * scale.astype(jnp.float32)).astype(xn_ext.dtype)


def a_layers(h, pool_prefix, pos, g_pool, w_pool, pool_scale, g_mlp, w_up, w_down):
    bufs = []
    for l in range(N_A_LAYERS):
        xn = rmsnorm(h, g_pool[l])
        ext = jnp.concatenate([pool_prefix[l].astype(xn.dtype), xn], axis=1)
        h = h + pool_mixer(ext, pos, w_pool[l], pool_scale[l])
        h = h + sq_relu_mlp(h, g_mlp[l], w_up[l], w_down[l])
        bufs.append(ext[:, -POOL_BUF:])
    return h, jnp.stack(bufs, axis=0)


def shared_kv(h, g_kv, w_k, w_v, g_k, w_f, b_f):
    B, T, _ = h.shape
    hn = rmsnorm(h, g_kv)
    k = rmsnorm((hn @ w_k).reshape(B, T, N_HEADS, HEAD_DIM), g_k)
    v = (hn @ w_v).reshape(B, T, N_HEADS, HEAD_DIM)
    logf = jax.nn.log_sigmoid((hn @ w_f).astype(jnp.float32) + b_f.astype(jnp.float32))
    return k, v, logf


def fox_attend(q, k, v, cq, ck, q_pos, k_pos):
    s = jnp.einsum('bqhd,bkhd->bhqk', q, k, preferred_element_type=jnp.float32) * (1.0 / math.sqrt(HEAD_DIM))
    decay = jnp.transpose(cq, (0, 2, 1))[..., :, None] - jnp.transpose(ck, (0, 2, 1))[..., None, :]
    mask = k_pos[None, :] <= q_pos[:, None]
    s = jnp.where(mask[None, None], s + decay, -jnp.inf)
    p = jax.nn.softmax(s, axis=-1)
    return jnp.einsum('bhqk,bkhd->bqhd', p.astype(v.dtype), v)


def fox_prompt(q, k, v, c):
    B, T, H, hd = q.shape
    nb = T // Q_BLOCK
    qb = q.reshape(B, nb, Q_BLOCK, H, hd).transpose(1, 0, 2, 3, 4)
    cb = c.reshape(B, nb, Q_BLOCK, H).transpose(1, 0, 2, 3)
    pb = jnp.arange(T, dtype=jnp.int32).reshape(nb, Q_BLOCK)
    kpos = jnp.arange(T, dtype=jnp.int32)
    out = lax.map(lambda a: fox_attend(a[0], k, v, a[1], c, a[2], kpos), (qb, cb, pb))
    return out.transpose(1, 0, 2, 3, 4).reshape(B, T, H, hd)


def b_layers(h, attend, g_attn, w_q, g_q, w_o, g_mlp, w_up, w_down):
    B, T, _ = h.shape
    for l in range(N_B_LAYERS):
        q = rmsnorm((rmsnorm(h, g_attn[l]) @ w_q[l]).reshape(B, T, N_HEADS, HEAD_DIM), g_q[l])
        o = attend(q).reshape(B, T, D_MODEL)
        h = h + o @ w_o[l]
        j = N_A_LAYERS + l
        h = h + sq_relu_mlp(h, g_mlp[j], w_up[j], w_down[j])
    return h


def setup_inputs(seed: int = 0) -> dict:
    key = jax.random.key(seed)
    ks = jax.random.split(key, 32)
    n_pages = PAST_LEN // PAGE_SIZE
    n_used = DEC_BATCH * n_pages
    n_pool = n_used + n_used // 4
    nrm = lambda k, shp, s: jax.random.normal(k, shp, jnp.float32) * s
    x_prompt = nrm(ks[0], (BATCH, SEQ, D_MODEL), 1.0)
    x_sample = nrm(ks[1], (DEC_BATCH, DEC_SEQ, D_MODEL), 1.0)
    state_pool = nrm(ks[2], (N_A_LAYERS, DEC_BATCH, POOL_BUF, D_MODEL), 1.0)
    cache_k = nrm(ks[3], (n_pool, PAGE_SIZE, N_HEADS, HEAD_DIM), 1.0)
    cache_v = nrm(ks[4], (n_pool, PAGE_SIZE, N_HEADS, HEAD_DIM), 1.0)
    cache_logf = jax.nn.log_sigmoid(FORGET_BIAS + nrm(ks[5], (n_pool, PAGE_SIZE, N_HEADS), 1.0))
    page_table = jax.random.permutation(ks[6], n_pool)[:n_used].reshape(DEC_BATCH, n_pages).astype(jnp.int32)
    g_pool = 1.0 + nrm(ks[7], (N_A_LAYERS, D_MODEL), 0.05)
    w_pool = nrm(ks[8], (N_A_LAYERS, N_POOL_GROUPS, POOL_GROUP_DIM, POOL_GROUP_DIM), POOL_GROUP_DIM ** -0.5)
    pool_scale = 1.0 + nrm(ks[9], (N_A_LAYERS, D_MODEL), 0.1)
    g_attn = 1.0 + nrm(ks[10], (N_B_LAYERS, D_MODEL), 0.05)
    w_q = nrm(ks[11], (N_B_LAYERS, D_MODEL, D_MODEL), D_MODEL ** -0.5)
    g_q = 1.0 + nrm(ks[12], (N_B_LAYERS, HEAD_DIM), 0.05)
    w_o = nrm(ks[13], (N_B_LAYERS, D_MODEL, D_MODEL), D_MODEL ** -0.5)
    g_kv = 1.0 + nrm(ks[14], (D_MODEL,), 0.05)
    w_k = nrm(ks[15], (D_MODEL, D_MODEL), D_MODEL ** -0.5)
    w_v = nrm(ks[16], (D_MODEL, D_MODEL), D_MODEL ** -0.5)
    g_k = 1.0 + nrm(ks[17], (HEAD_DIM,), 0.05)
    w_f = nrm(ks[18], (D_MODEL, N_HEADS), D_MODEL ** -0.5)
    b_f = FORGET_BIAS + nrm(ks[19], (N_HEADS,), 0.1)
    g_mlp = 1.0 + nrm(ks[20], (DEPTH, D_MODEL), 0.05)
    w_up = nrm(ks[21], (DEPTH, D_MODEL, D_FF), D_MODEL ** -0.5)
    w_down = nrm(ks[22], (DEPTH, D_FF, D_MODEL), D_FF ** -0.5)
    return {"x_prompt": x_prompt, "x_sample": x_sample, "state_pool": state_pool,
            "cache_k": cache_k, "cache_v": cache_v, "cache_logf": cache_logf, "page_table": page_table,
            "g_pool": g_pool, "w_pool": w_pool, "pool_scale": pool_scale,
            "g_attn": g_attn, "w_q": w_q, "g_q": g_q, "w_o": w_o,
            "g_kv": g_kv, "w_k": w_k, "w_v": w_v, "g_k": g_k, "w_f": w_f, "b_f": b_f,
            "g_mlp": g_mlp, "w_up": w_up, "w_down": w_down}


def reference(x_prompt, x_sample, state_pool, cache_k, cache_v, cache_logf, page_table,
              g_pool, w_pool, pool_scale, g_attn, w_q, g_q, w_o,
              g_kv, w_k, w_v, g_k, w_f, b_f, g_mlp, w_up, w_down):
    Bp, Tp, _ = x_prompt.shape
    pos_p = jnp.arange(Tp, dtype=jnp.int32)
    zero_prefix = jnp.zeros((N_A_LAYERS, Bp, POOL_BUF, D_MODEL), x_prompt.dtype)
    h_p, pool_state_prompt = a_layers(x_prompt, zero_prefix, pos_p, g_pool, w_pool, pool_scale, g_mlp, w_up, w_down)
    k_p, v_p, logf_p = shared_kv(h_p, g_kv, w_k, w_v, g_k, w_f, b_f)
    c_p = jnp.cumsum(logf_p, axis=1)
    y_prompt = b_layers(h_p, lambda q: fox_prompt(q, k_p, v_p, c_p),
                        g_attn, w_q, g_q, w_o, g_mlp, w_up, w_down)

    Bs, Ts, _ = x_sample.shape
    past = page_table.shape[1] * PAGE_SIZE
    pos_s = past + jnp.arange(Ts, dtype=jnp.int32)
    h_s, pool_state_sample = a_layers(x_sample, state_pool, pos_s, g_pool, w_pool, pool_scale, g_mlp, w_up, w_down)
    k_s, v_s, logf_s = shared_kv(h_s, g_kv, w_k, w_v, g_k, w_f, b_f)
    k_past = cache_k[page_table].reshape(Bs, past, N_HEADS, HEAD_DIM)
    v_past = cache_v[page_table].reshape(Bs, past, N_HEADS, HEAD_DIM)
    logf_past = cache_logf[page_table].reshape(Bs, past, N_HEADS).astype(jnp.float32)
    k_all = jnp.concatenate([k_past.astype(k_s.dtype), k_s], axis=1)
    v_all = jnp.concatenate([v_past.astype(v_s.dtype), v_s], axis=1)
    c_all = jnp.cumsum(jnp.concatenate([logf_past, logf_s], axis=1), axis=1)
    c_new = c_all[:, past:]
    k_pos = jnp.arange(past + Ts, dtype=jnp.int32)
    y_sample = b_layers(h_s, lambda q: fox_attend(q, k_all, v_all, c_new, c_all, pos_s, k_pos),
                        g_attn, w_q, g_q, w_o, g_mlp, w_up, w_down)

    logf_prompt = logf_p.astype(cache_logf.dtype)
    logf_sample = logf_s.astype(cache_logf.dtype)
    return (y_prompt, y_sample, pool_state_prompt, pool_state_sample,
            k_p, v_p, logf_prompt, k_s, v_s, logf_sample)
```

```python
import functools

import jax
import jax.numpy as jnp
from jax import lax
from jax.experimental import pallas as pl
from jax.experimental.pallas import tpu as pltpu

F32 = jnp.float32
BF16 = jnp.bfloat16

EPS = 1e-6
POOL_WINDOWS = (2, 4, 8, 16)
POOL_BUF = max(POOL_WINDOWS) - 1
N_HEADS = 16
LANES = 128
HALO = 16
VMEM_LIMIT = 56 * 1024 * 1024
NEG_BIG = -1e30

TOK_BLOCK = 512
FF_CHUNK = 1024
ATT_TQ = 256
ATT_TK = 512


def _dot(a, b):
    return jnp.dot(a, b, preferred_element_type=F32)


def _dot_nt(a, b):
    return lax.dot_general(a, b, (((1,), (1,)), ((), ())), preferred_element_type=F32)


def _rms(x, g):
    ms = jnp.mean(x * x, axis=-1, keepdims=True)
    return x * lax.rsqrt(ms + EPS) * g


def _split2(x):
    hi = x.astype(BF16)
    lo = (x - hi.astype(F32)).astype(BF16)
    return hi, lo


def _split3(x):
    hi = x.astype(BF16)
    r1 = x - hi.astype(F32)
    mid = r1.astype(BF16)
    lo = (r1 - mid.astype(F32)).astype(BF16)
    return hi, mid, lo


def _head_rms(x, g_tiled, e, et, head_dim):
    hi, lo = _split2(x * x)
    ss = _dot(hi, e) + _dot(lo, e)
    r = lax.rsqrt(ss * (1.0 / head_dim) + EPS)
    rh, rl = _split2(r)
    return x * (_dot(rh, et) + _dot(rl, et)) * g_tiled


def _const_spec(shape):
    return pl.BlockSpec(shape, lambda *_: (0,) * len(shape), pipeline_mode=pl.Buffered(1))


def _params(*sem):
    return pltpu.CompilerParams(dimension_semantics=sem, vmem_limit_bytes=VMEM_LIMIT)


def _pool_prompt_kernel(x_ref, halo_ref, g_ref, w_ref, sc_ref, o_ref, st_ref, ext_ref, *, tb, gdim):
    i = pl.program_id(1)
    g = g_ref[...]
    x = x_ref[...]
    xn = _rms(x, g)
    hn = _rms(halo_ref[...], g)
    ext_ref[0:HALO, :] = jnp.where(i > 0, hn, 0.0)
    ext_ref[HALO:, :] = xn
    pos = i * tb + lax.broadcasted_iota(jnp.int32, (tb, 1), 0)
    sc = sc_ref[...]
    for gi, w in enumerate(POOL_WINDOWS):
        sl = slice(gi * gdim, (gi + 1) * gdim)
        acc = xn[:, sl]
        for j in range(1, w):
            acc = acc + ext_ref[HALO - j:HALO - j + tb, sl]
        cnt = jnp.minimum(pos + 1, w).astype(F32)
        pooled = acc / cnt - xn[:, sl]
        mixed = _dot(pooled.astype(BF16), w_ref[gi]) * sc[:, sl]
        o_ref[:, sl] = x[:, sl] + mixed
    st_ref[...] = xn[tb - HALO:, :]


def _pool_prompt(h, g, w, sc):
    B, T, D = h.shape
    tb = TOK_BLOCK
    gdim = D // len(POOL_WINDOWS)
    r = tb // HALO
    return pl.pallas_call(
        functools.partial(_pool_prompt_kernel, tb=tb, gdim=gdim),
        grid=(B, T // tb),
        in_specs=[
            pl.BlockSpec((None, tb, D), lambda b, i: (b, i, 0)),
            pl.BlockSpec((None, HALO, D), lambda b, i: (b, jnp.maximum(i * r - 1, 0), 0)),
            _const_spec((1, D)),
            _const_spec((len(POOL_WINDOWS), gdim, gdim)),
            _const_spec((1, D)),
        ],
        out_specs=[
            pl.BlockSpec((None, tb, D), lambda b, i: (b, i, 0)),
            pl.BlockSpec((None, HALO, D), lambda b, i: (b, 0, 0)),
        ],
        out_shape=[jax.ShapeDtypeStruct((B, T, D), F32), jax.ShapeDtypeStruct((B, HALO, D), F32)],
        scratch_shapes=[pltpu.VMEM((tb + HALO, D), F32)],
        compiler_params=_params("arbitrary", "arbitrary"),
        name="pool_prompt",
    )(h, h, g, w, sc)


def _pool_sample_kernel(x_ref, pf_ref, g_ref, w_ref, sc_ref, o_ref, st_ref, *, ts, bb, gdim, cnts):
    g = g_ref[...]
    sc = sc_ref[...]
    xs = [x_ref[t] for t in range(ts)]
    xn = [_rms(x, g) for x in xs]
    ext = [pf_ref[r] for r in range(POOL_BUF)] + xn
    for gi, w in enumerate(POOL_WINDOWS):
        sl = slice(gi * gdim, (gi + 1) * gdim)
        pooled = []
        for t in range(ts):
            acc = ext[POOL_BUF + t][:, sl]
            for j in range(1, w):
                acc = acc + ext[POOL_BUF + t - j][:, sl]
            pooled.append(acc / cnts[t][gi] - xn[t][:, sl])
        mixed = _dot(jnp.concatenate(pooled, axis=0).astype(BF16), w_ref[gi]) * sc[:, sl]
        for t in range(ts):
            o_ref[t, :, sl] = xs[t][:, sl] + mixed[t * bb:(t + 1) * bb]
    for r in range(POOL_BUF):
        st_ref[r] = ext[ts + r]


def _pool_sample(h_tm, prefix_tm, g, w, sc, past):
    ts, bs, D = h_tm.shape
    bb = 32
    gdim = D // len(POOL_WINDOWS)
    cnts = tuple(tuple(float(min(past + t + 1, w)) for w in POOL_WINDOWS) for t in range(ts))
    return pl.pallas_call(
        functools.partial(_pool_sample_kernel, ts=ts, bb=bb, gdim=gdim, cnts=cnts),
        grid=(bs // bb,),
        in_specs=[
            pl.BlockSpec((ts, bb, D), lambda i: (0, i, 0)),
            pl.BlockSpec((POOL_BUF, bb, D), lambda i: (0, i, 0)),
            _const_spec((1, D)),
            _const_spec((len(POOL_WINDOWS), gdim, gdim)),
            _const_spec((1, D)),
        ],
        out_specs=[
            pl.BlockSpec((ts, bb, D), lambda i: (0, i, 0)),
            pl.BlockSpec((POOL_BUF, bb, D), lambda i: (0, i, 0)),
        ],
        out_shape=[jax.ShapeDtypeStruct((ts, bs, D), F32), jax.ShapeDtypeStruct((POOL_BUF, bs, D), F32)],
        compiler_params=_params("arbitrary"),
        name="pool_sample",
    )(h_tm, prefix_tm, g, w, sc)


def _mlp_kernel(*refs, has_o, n_chunks):
    if has_o:
        h_ref, o_ref, wo_ref, g_ref, wu_ref, wd_ref, out_ref = refs
        h = h_ref[...] + _dot(o_ref[...], wo_ref[...])
    else:
        h_ref, g_ref, wu_ref, wd_ref, out_ref = refs
        h = h_ref[...]
    xn = _rms(h, g_ref[...]).astype(BF16)
    acc = h
    for c in range(n_chunks):
        cs = slice(c * FF_CHUNK, (c + 1) * FF_CHUNK)
        a = jnp.maximum(_dot(xn, wu_ref[:, cs]), 0.0)
        acc = acc + _dot((a * a).astype(BF16), wd_ref[cs, :])
    out_ref[...] = acc


def _mlp(h, g, w_up, w_down, o=None, w_o=None):
    N, D = h.shape
    F = w_up.shape[1]
    tm = TOK_BLOCK
    tok = pl.BlockSpec((tm, D), lambda i: (i, 0))
    has_o = o is not None
    in_specs = [tok]
    args = [h]
    if has_o:
        in_specs += [tok, _const_spec((D, D))]
        args += [o, w_o]
    in_specs += [_const_spec((1, D)), _const_spec((D, F)), _const_spec((F, D))]
    args += [g, w_up, w_down]
    return pl.pallas_call(
        functools.partial(_mlp_kernel, has_o=has_o, n_chunks=F // FF_CHUNK),
        grid=(N // tm,),
        in_specs=in_specs,
        out_specs=tok,
        out_shape=jax.ShapeDtypeStruct((N, D), F32),
        compiler_params=_params("arbitrary"),
        name="mlp_o" if has_o else "mlp",
    )(*args)


def _log_sigmoid(z):
    return -(jnp.maximum(-z, 0.0) + jnp.log1p(jnp.exp(-jnp.abs(z))))


def _cumsum_t(lf, eye, tri, n_heads):
    c = None
    for part in _split3(lf):
        pt = _dot_nt(eye, part)[:n_heads]
        term = _dot(pt.astype(BF16), tri)
        c = term if c is None else c + term
    return c


def _kv_kernel(h_ref, g_ref, wk_ref, wv_ref, gk_ref, wf_ref, bf_ref, e_ref, et_ref, eye_ref, tri_ref,
               k_ref, v_ref, lf_ref, kb_ref, vb_ref, ct_ref, carry_ref, *, nb, head_dim):
    i = pl.program_id(0)
    hn = _rms(h_ref[...], g_ref[...]).astype(BF16)
    k = _head_rms(_dot(hn, wk_ref[...]), gk_ref[...], e_ref[...], et_ref[...], head_dim)
    v = _dot(hn, wv_ref[...])
    lf = _log_sigmoid(_dot(hn, wf_ref[...]) + bf_ref[...])
    k_ref[...] = k
    v_ref[...] = v
    kb_ref[...] = k.astype(BF16)
    vb_ref[...] = v.astype(BF16)
    lf_ref[...] = lf[:, :N_HEADS]

    @pl.when(i % nb == 0)
    def _():
        carry_ref[...] = jnp.zeros_like(carry_ref)

    lane = lax.broadcasted_iota(jnp.int32, lf.shape, 1)
    c = _cumsum_t(jnp.where(lane < N_HEADS, lf, 0.0), eye_ref[...], tri_ref[...], N_HEADS) + carry_ref[...]
    ct_ref[...] = c
    carry_ref[...] = c[:, c.shape[1] - 1:]


def _kv(h, rows_per_seq, g, wk, wv, gk_t, wf_p, bf_p, e, et):
    N, D = h.shape
    tm = min(TOK_BLOCK, rows_per_seq)
    nb = rows_per_seq // tm
    tok = pl.BlockSpec((tm, D), lambda i: (i, 0))
    eye = jnp.eye(LANES, dtype=BF16)
    tri = (jnp.arange(tm)[:, None] <= jnp.arange(tm)[None, :]).astype(BF16)
    return pl.pallas_call(
        functools.partial(_kv_kernel, nb=nb, head_dim=D // N_HEADS),
        grid=(N // tm,),
        in_specs=[tok, _const_spec((1, D)), _const_spec((D, D)), _const_spec((D, D)), _const_spec((1, D)),
                  _const_spec((D, LANES)), _const_spec((1, LANES)), _const_spec((D, LANES)),
                  _const_spec((LANES, D)), _const_spec((LANES, LANES)), _const_spec((tm, tm))],
        out_specs=[tok, tok, pl.BlockSpec((tm, N_HEADS), lambda i: (i, 0)), tok, tok,
                   pl.BlockSpec((None, N_HEADS, tm), lambda i: (i // nb, 0, i % nb))],
        out_shape=[jax.ShapeDtypeStruct((N, D), F32), jax.ShapeDtypeStruct((N, D), F32),
                   jax.ShapeDtypeStruct((N, N_HEADS), F32),
                   jax.ShapeDtypeStruct((N, D), BF16), jax.ShapeDtypeStruct((N, D), BF16),
                   jax.ShapeDtypeStruct((N // rows_per_seq, N_HEADS, rows_per_seq), F32)],
        scratch_shapes=[pltpu.VMEM((N_HEADS, 1), F32)],
        compiler_params=_params("arbitrary"),
        name="shared_kv",
    )(h, g, wk, wv, gk_t, wf_p, bf_p, e, et, eye, tri)


def _q_kernel(h_ref, g_ref, wq_ref, gq_ref, e_ref, et_ref, q_ref, *, head_dim):
    xn = _rms(h_ref[...], g_ref[...]).astype(BF16)
    q = _head_rms(_dot(xn, wq_ref[...]), gq_ref[...], e_ref[...], et_ref[...], head_dim)
    q_ref[...] = (q * (head_dim ** -0.5)).astype(BF16)


def _qproj(h, g, wq, gq_t, e, et):
    N, D = h.shape
    tm = TOK_BLOCK
    tok = pl.BlockSpec((tm, D), lambda i: (i, 0))
    return pl.pallas_call(
        functools.partial(_q_kernel, head_dim=D // N_HEADS),
        grid=(N // tm,),
        in_specs=[tok, _const_spec((1, D)), _const_spec((D, D)), _const_spec((1, D)),
                  _const_spec((D, LANES)), _const_spec((LANES, D))],
        out_specs=tok,
        out_shape=jax.ShapeDtypeStruct((N, D), BF16),
        compiler_params=_params("arbitrary"),
        name="q_proj",
    )(h, g, wq, gq_t, e, et)


def _attn_prompt_kernel(q_ref, k_ref, v_ref, c_ref, o_ref, *, tq, tk, head_dim):
    qi = pl.program_id(2)
    q2 = q_ref[...]
    lane = lax.broadcasted_iota(jnp.int32, q2.shape, 1)
    zero = jnp.zeros_like(q2)
    qh = (jnp.where(lane < head_dim, q2, zero), jnp.where(lane >= head_dim, q2, zero))
    row_pos = qi * tq + lax.broadcasted_iota(jnp.int32, (tq, tk), 0)
    col_iota = lax.broadcasted_iota(jnp.int32, (tq, tk), 1)
    n_blocks = (qi * tq + tq + tk - 1) // tk

    def body(j, carry):
        off = pl.multiple_of(j * tk, tk)
        kb = k_ref[pl.ds(off, tk), :]
        vb = v_ref[pl.ds(off, tk), :]
        keep = (off + col_iota) <= row_pos
        out = []
        for x in range(2):
            m, l, acc = carry[x]
            s = _dot_nt(qh[x], kb) - c_ref[x:x + 1, pl.ds(off, tk)]
            s = jnp.where(keep, s, NEG_BIG)
            m_new = jnp.maximum(m, jnp.max(s, axis=1, keepdims=True))
            alpha = jnp.exp(m - m_new)
            p = jnp.exp(s - m_new)
            l = alpha * l + jnp.sum(p, axis=1, keepdims=True)
            acc = alpha * acc + _dot(p.astype(BF16), vb)
            out.append((m_new, l, acc))
        return tuple(out)

    init = tuple((jnp.full((tq, 1), NEG_BIG, F32), jnp.zeros((tq, 1), F32), jnp.zeros((tq, LANES), F32))
                 for _ in range(2))
    (_, l0, a0), (_, l1, a1) = lax.fori_loop(0, n_blocks, body, init)
    o_ref[...] = jnp.where(lane < head_dim, a0 / l0, a1 / l1).astype(o_ref.dtype)


def _attn_prompt(q, kb, vb, ct):
    B, T, D = q.shape
    head_dim = D // N_HEADS
    pairs = D // LANES
    ct4 = ct.reshape(B, pairs, N_HEADS // pairs, T)
    return pl.pallas_call(
        functools.partial(_attn_prompt_kernel, tq=ATT_TQ, tk=ATT_TK, head_dim=head_dim),
        grid=(B, pairs, T // ATT_TQ),
        in_specs=[
            pl.BlockSpec((None, ATT_TQ, LANES), lambda b, p, i: (b, i, p)),
            pl.BlockSpec((None, T, LANES), lambda b, p, i: (b, 0, p)),
            pl.BlockSpec((None, T, LANES), lambda b, p, i: (b, 0, p)),
            pl.BlockSpec((None, None, N_HEADS // pairs, T), lambda b, p, i: (b, p, 0, 0)),
        ],
        out_specs=pl.BlockSpec((None, ATT_TQ, LANES), lambda b, p, i: (b, i, p)),
        out_shape=jax.ShapeDtypeStruct((B, T, D), BF16),
        compiler_params=_params("arbitrary", "arbitrary", "arbitrary"),
        name="attn_prompt",
    )(q, kb, vb, ct4)


def _cum_pages_kernel(pt_ref, *refs, n_pages, page):
    del pt_ref
    page_refs = refs[:n_pages + 1]
    ct_ref, pad_ref = refs[n_pages + 1:]
    row = lax.broadcasted_iota(jnp.int32, (LANES, LANES), 0)
    col = lax.broadcasted_iota(jnp.int32, (LANES, LANES), 1)
    eye = (row == col).astype(BF16)
    tri = (row <= col).astype(BF16)
    pad_ref[...] = jnp.zeros_like(pad_ref)
    carry = jnp.zeros((N_HEADS, 1), F32)
    for j, r in enumerate(page_refs):
        pad_ref[:, 0:N_HEADS] = r[...]
        c = _cumsum_t(pad_ref[...], eye, tri, N_HEADS) + carry
        ct_ref[:, j * page:(j + 1) * page] = c
        carry = c[:, page - 1:]


def _cum_pages(cache_logf, page_table, logf_new_pad):
    bs, n_pages = page_table.shape
    page = cache_logf.shape[1]
    assert page == LANES
    in_specs = [pl.BlockSpec((None, page, N_HEADS), lambda b, pt, j=j: (pt[b, j], 0, 0)) for j in range(n_pages)]
    in_specs.append(pl.BlockSpec((None, page, N_HEADS), lambda b, pt: (b, 0, 0)))
    width = (n_pages + 1) * page
    return pl.pallas_call(
        functools.partial(_cum_pages_kernel, n_pages=n_pages, page=page),
        grid_spec=pltpu.PrefetchScalarGridSpec(
            num_scalar_prefetch=1,
            grid=(bs,),
            in_specs=in_specs,
            out_specs=pl.BlockSpec((None, N_HEADS, width), lambda b, pt: (b, 0, 0)),
            scratch_shapes=[pltpu.VMEM((page, LANES), F32)],
        ),
        out_shape=jax.ShapeDtypeStruct((bs, N_HEADS, width), F32),
        compiler_params=_params("arbitrary"),
        name="cum_pages",
    )(page_table, *([cache_logf] * n_pages), logf_new_pad)


def _attn_sample_kernel(pt_ref, q_ref, ct_ref, kn_ref, vn_ref, *refs, n_pages, page, ts, head_dim, new_pad):
    del pt_ref
    k_refs = refs[:n_pages]
    v_refs = refs[n_pages:2 * n_pages]
    o_ref = refs[2 * n_pages]
    D = q_ref.shape[1]
    rows = LANES
    reps = rows // N_HEADS
    hrow = lax.broadcasted_iota(jnp.int32, (N_HEADS, D), 0)
    hcol = lax.broadcasted_iota(jnp.int32, (N_HEADS, D), 1) // head_dim
    head_mask = hrow == hcol
    q = q_ref[...].astype(F32)
    zq = jnp.zeros((N_HEADS, D), F32)
    qbd = jnp.concatenate(
        [jnp.where(head_mask, jnp.broadcast_to(q[t:t + 1, :], (N_HEADS, D)), zq) for t in range(ts)]
        + [zq] * (reps - ts), axis=0).astype(BF16)

    def update(state, s, vb):
        m, l, acc = state
        m_new = jnp.maximum(m, jnp.max(s, axis=1, keepdims=True))
        alpha = jnp.exp(m - m_new)
        p = jnp.exp(s - m_new)
        l = alpha * l + jnp.sum(p, axis=1, keepdims=True)
        acc = alpha * acc + _dot(p.astype(BF16), vb)
        return m_new, l, acc

    def sub_c(s, c):
        n = s.shape[1]
        return (s.reshape(reps, N_HEADS, n) - c[None]).reshape(rows, n)

    state = (jnp.full((rows, 1), NEG_BIG, F32), jnp.zeros((rows, 1), F32), jnp.zeros((rows, D), F32))
    for j in range(n_pages):
        kb = k_refs[j][...].astype(BF16)
        vb = v_refs[j][...].astype(BF16)
        s = sub_c(_dot_nt(qbd, kb), ct_ref[:, j * page:(j + 1) * page])
        state = update(state, s, vb)
    s = sub_c(_dot_nt(qbd, kn_ref[...]), ct_ref[:, n_pages * page:n_pages * page + new_pad])
    t_row = lax.broadcasted_iota(jnp.int32, (rows, new_pad), 0) // N_HEADS
    t_key = lax.broadcasted_iota(jnp.int32, (rows, new_pad), 1)
    s = jnp.where(t_key <= t_row, s, NEG_BIG)
    _, l, acc = update(state, s, vn_ref[...])
    o = acc / l
    zo = jnp.zeros((N_HEADS, D), F32)
    for t in range(ts):
        blk = jnp.where(head_mask, o[t * N_HEADS:(t + 1) * N_HEADS, :], zo)
        o_ref[t:t + 1, :] = jnp.sum(blk, axis=0, keepdims=True).astype(o_ref.dtype)


def _attn_sample(q, ct, k_new, v_new, cache_k, cache_v, page_table):
    bs, ts, D = q.shape
    n_pages = page_table.shape[1]
    n_pool, page = cache_k.shape[0], cache_k.shape[1]
    new_pad = k_new.shape[1]
    ck = cache_k.reshape(n_pool, page, D)
    cv = cache_v.reshape(n_pool, page, D)
    seq = lambda n: pl.BlockSpec((None, n, D), lambda b, pt: (b, 0, 0))
    pages = [pl.BlockSpec((None, page, D), lambda b, pt, j=j: (pt[b, j], 0, 0)) for j in range(n_pages)]
    return pl.pallas_call(
        functools.partial(_attn_sample_kernel, n_pages=n_pages, page=page, ts=ts,
                          head_dim=D // N_HEADS, new_pad=new_pad),
        grid_spec=pltpu.PrefetchScalarGridSpec(
            num_scalar_prefetch=1,
            grid=(bs,),
            in_specs=[seq(ts), pl.BlockSpec((None, N_HEADS, ct.shape[2]), lambda b, pt: (b, 0, 0)),
                      seq(new_pad), seq(new_pad)] + pages + pages,
            out_specs=seq(ts),
        ),
        out_shape=jax.ShapeDtypeStruct((bs, ts, D), BF16),
        compiler_params=_params("arbitrary"),
        name="attn_sample",
    )(page_table, q, ct, k_new, v_new, *([ck] * n_pages), *([cv] * n_pages))


def kernel(x_prompt, x_sample, state_pool, cache_k, cache_v, cache_logf, page_table, g_pool, w_pool, pool_scale,
           g_attn, w_q, g_q, w_o, g_kv, w_k, w_v, g_k, w_f, b_f, g_mlp, w_up, w_down):
    Bp, Tp, D = x_prompt.shape
    Bs, Ts, _ = x_sample.shape
    n_a = g_pool.shape[0]
    n_b = g_attn.shape[0]
    head_dim = D // N_HEADS
    past = page_table.shape[1] * cache_k.shape[1]
    new_pad = 16

    row = lambda a: a.reshape(1, -1).astype(F32)
    w_pool_b = w_pool.astype(BF16)
    w_up_b, w_down_b = w_up.astype(BF16), w_down.astype(BF16)
    w_q_b, w_o_b = w_q.astype(BF16), w_o.astype(BF16)
    w_k_b, w_v_b = w_k.astype(BF16), w_v.astype(BF16)
    w_f_p = jnp.pad(w_f, ((0, 0), (0, LANES - N_HEADS))).astype(BF16)
    b_f_p = jnp.pad(b_f.astype(F32), (0, LANES - N_HEADS)).reshape(1, LANES)
    e = (jnp.arange(D)[:, None] // head_dim == jnp.arange(LANES)[None, :]).astype(BF16)
    et = e.T
    g_k_t = row(jnp.tile(g_k, N_HEADS))

    h_p = x_prompt
    h_s = jnp.transpose(x_sample, (1, 0, 2))
    st_p, st_s = [], []
    for l in range(n_a):
        args = (row(g_pool[l]), w_pool_b[l], row(pool_scale[l]))
        h_p, tail = _pool_prompt(h_p, *args)
        st_p.append(tail[:, HALO - POOL_BUF:])
        h_p = _mlp(h_p.reshape(Bp * Tp, D), row(g_mlp[l]), w_up_b[l], w_down_b[l]).reshape(Bp, Tp, D)
        h_s, st = _pool_sample(h_s, jnp.transpose(state_pool[l], (1, 0, 2)), *args, past)
        st_s.append(jnp.transpose(st, (1, 0, 2)))
        h_s = _mlp(h_s.reshape(Ts * Bs, D), row(g_mlp[l]), w_up_b[l], w_down_b[l]).reshape(Ts, Bs, D)
    pool_state_prompt = jnp.stack(st_p, axis=0)
    pool_state_sample = jnp.stack(st_s, axis=0)

    h_p = h_p.reshape(Bp * Tp, D)
    h_s = jnp.transpose(h_s, (1, 0, 2)).reshape(Bs * Ts, D)

    kv_w = (row(g_kv), w_k_b, w_v_b, g_k_t, w_f_p, b_f_p, e, et)
    k_p, v_p, lf_p, kb_p, vb_p, ct_p = _kv(h_p, Tp, *kv_w)
    k_s, v_s, lf_s, kb_s, vb_s, _ = _kv(h_s, Bs * Ts, *kv_w)
    pad_new = lambda a, n: jnp.pad(a.reshape(Bs, Ts, -1), ((0, 0), (0, n - Ts), (0, 0)))
    ct_s = _cum_pages(cache_logf, page_table, pad_new(lf_s, cache_logf.shape[1]))
    kn, vn = pad_new(kb_s, new_pad), pad_new(vb_s, new_pad)

    for l in range(n_b):
        j = n_a + l
        g_q_t = row(jnp.tile(g_q[l], N_HEADS))
        q_p = _qproj(h_p, row(g_attn[l]), w_q_b[l], g_q_t, e, et)
        o_p = _attn_prompt(q_p.reshape(Bp, Tp, D), kb_p.reshape(Bp, Tp, D), vb_p.reshape(Bp, Tp, D), ct_p)
        h_p = _mlp(h_p, row(g_mlp[j]), w_up_b[j], w_down_b[j], o_p.reshape(Bp * Tp, D), w_o_b[l])
        q_s = _qproj(h_s, row(g_attn[l]), w_q_b[l], g_q_t, e, et)
        o_s = _attn_sample(q_s.reshape(Bs, Ts, D), ct_s, kn, vn, cache_k, cache_v, page_table)
        h_s = _mlp(h_s, row(g_mlp[j]), w_up_b[j], w_down_b[j], o_s.reshape(Bs * Ts, D), w_o_b[l])

    hd4 = lambda a, b, t: a.reshape(b, t, N_HEADS, head_dim)
    return (h_p.reshape(Bp, Tp, D), h_s.reshape(Bs, Ts, D), pool_state_prompt, pool_state_sample,
            hd4(k_p, Bp, Tp), hd4(v_p, Bp, Tp), lf_p.reshape(Bp, Tp, N_HEADS).astype(cache_logf.dtype),
            hd4(k_s, Bs, Ts), hd4(v_s, Bs, Ts), lf_s.reshape(Bs, Ts, N_HEADS).astype(cache_logf.dtype))
```

```python
import functools

import jax
import jax.numpy as jnp
from jax import lax
from jax.experimental import pallas as pl
from jax.experimental.pallas import tpu as pltpu

F32 = jnp.float32
BF16 = jnp.bfloat16

EPS = 1e-6
POOL_WINDOWS = (2, 4, 8, 16)
POOL_BUF = max(POOL_WINDOWS) - 1
N_HEADS = 16
LANES = 128
HALO = 16
VMEM_LIMIT = 56 * 1024 * 1024
NEG_BIG = -1e30

TOK_BLOCK = 512
FF_CHUNK = 1024
ATT_BLOCK = 512
ATT_HEADS = 4
V_EXTRA = 16
LOG2E = 1.4426950408889634


def _dot(a, b):
    return jnp.dot(a, b, preferred_element_type=F32)


def _dot_nt(a, b):
    return lax.dot_general(a, b, (((1,), (1,)), ((), ())), preferred_element_type=F32)


def _rms(x, g):
    ms = jnp.mean(x * x, axis=-1, keepdims=True)
    return x * lax.rsqrt(ms + EPS) * g


def _split2(x):
    hi = x.astype(BF16)
    lo = (x - hi.astype(F32)).astype(BF16)
    return hi, lo


def _split3(x):
    hi = x.astype(BF16)
    r1 = x - hi.astype(F32)
    mid = r1.astype(BF16)
    lo = (r1 - mid.astype(F32)).astype(BF16)
    return hi, mid, lo


def _head_rms(x, g_tiled, e, et, head_dim):
    hi, lo = _split2(x * x)
    ss = _dot(hi, e) + _dot(lo, e)
    r = lax.rsqrt(ss * (1.0 / head_dim) + EPS)
    rh, rl = _split2(r)
    return x * (_dot(rh, et) + _dot(rl, et)) * g_tiled


def _const_spec(shape):
    return pl.BlockSpec(shape, lambda *_: (0,) * len(shape), pipeline_mode=pl.Buffered(1))


def _params(*sem):
    return pltpu.CompilerParams(dimension_semantics=sem, vmem_limit_bytes=VMEM_LIMIT)


def _pool_prompt_kernel(x_ref, halo_ref, g_ref, w_ref, sc_ref, o_ref, st_ref, ext_ref, *, tb, gdim):
    i = pl.program_id(1)
    g = g_ref[...]
    x = x_ref[...]
    xn = _rms(x, g)
    hn = _rms(halo_ref[...], g)
    ext_ref[0:HALO, :] = jnp.where(i > 0, hn, 0.0)
    ext_ref[HALO:, :] = xn
    pos = i * tb + lax.broadcasted_iota(jnp.int32, (tb, 1), 0)
    sc = sc_ref[...]
    for gi, w in enumerate(POOL_WINDOWS):
        sl = slice(gi * gdim, (gi + 1) * gdim)
        acc = xn[:, sl]
        for j in range(1, w):
            acc = acc + ext_ref[HALO - j:HALO - j + tb, sl]
        cnt = jnp.minimum(pos + 1, w).astype(F32)
        pooled = acc / cnt - xn[:, sl]
        mixed = _dot(pooled.astype(BF16), w_ref[gi]) * sc[:, sl]
        o_ref[:, sl] = x[:, sl] + mixed
    st_ref[...] = xn[tb - HALO:, :]


def _pool_prompt(h, g, w, sc):
    B, T, D = h.shape
    tb = TOK_BLOCK
    gdim = D // len(POOL_WINDOWS)
    r = tb // HALO
    return pl.pallas_call(
        functools.partial(_pool_prompt_kernel, tb=tb, gdim=gdim),
        grid=(B, T // tb),
        in_specs=[
            pl.BlockSpec((None, tb, D), lambda b, i: (b, i, 0)),
            pl.BlockSpec((None, HALO, D), lambda b, i: (b, jnp.maximum(i * r - 1, 0), 0)),
            _const_spec((1, D)),
            _const_spec((len(POOL_WINDOWS), gdim, gdim)),
            _const_spec((1, D)),
        ],
        out_specs=[
            pl.BlockSpec((None, tb, D), lambda b, i: (b, i, 0)),
            pl.BlockSpec((None, HALO, D), lambda b, i: (b, 0, 0)),
        ],
        out_shape=[jax.ShapeDtypeStruct((B, T, D), F32), jax.ShapeDtypeStruct((B, HALO, D), F32)],
        scratch_shapes=[pltpu.VMEM((tb + HALO, D), F32)],
        compiler_params=_params("arbitrary", "arbitrary"),
        name="pool_prompt",
    )(h, h, g, w, sc)


def _pool_sample_kernel(x_ref, pf_ref, g_ref, w_ref, sc_ref, o_ref, st_ref, *, ts, bb, gdim, cnts):
    g = g_ref[...]
    sc = sc_ref[...]
    xs = [x_ref[t] for t in range(ts)]
    xn = [_rms(x, g) for x in xs]
    ext = [pf_ref[r] for r in range(POOL_BUF)] + xn
    for gi, w in enumerate(POOL_WINDOWS):
        sl = slice(gi * gdim, (gi + 1) * gdim)
        pooled = []
        for t in range(ts):
            acc = ext[POOL_BUF + t][:, sl]
            for j in range(1, w):
                acc = acc + ext[POOL_BUF + t - j][:, sl]
            pooled.append(acc / cnts[t][gi] - xn[t][:, sl])
        mixed = _dot(jnp.concatenate(pooled, axis=0).astype(BF16), w_ref[gi]) * sc[:, sl]
        for t in range(ts):
            o_ref[t, :, sl] = xs[t][:, sl] + mixed[t * bb:(t + 1) * bb]
    for r in range(POOL_BUF):
        st_ref[r] = ext[ts + r]


def _pool_sample(h_tm, prefix_tm, g, w, sc, past):
    ts, bs, D = h_tm.shape
    bb = 32
    gdim = D // len(POOL_WINDOWS)
    cnts = tuple(tuple(float(min(past + t + 1, w)) for w in POOL_WINDOWS) for t in range(ts))
    return pl.pallas_call(
        functools.partial(_pool_sample_kernel, ts=ts, bb=bb, gdim=gdim, cnts=cnts),
        grid=(bs // bb,),
        in_specs=[
            pl.BlockSpec((ts, bb, D), lambda i: (0, i, 0)),
            pl.BlockSpec((POOL_BUF, bb, D), lambda i: (0, i, 0)),
            _const_spec((1, D)),
            _const_spec((len(POOL_WINDOWS), gdim, gdim)),
            _const_spec((1, D)),
        ],
        out_specs=[
            pl.BlockSpec((ts, bb, D), lambda i: (0, i, 0)),
            pl.BlockSpec((POOL_BUF, bb, D), lambda i: (0, i, 0)),
        ],
        out_shape=[jax.ShapeDtypeStruct((ts, bs, D), F32), jax.ShapeDtypeStruct((POOL_BUF, bs, D), F32)],
        compiler_params=_params("arbitrary"),
        name="pool_sample",
    )(h_tm, prefix_tm, g, w, sc)


def _mlp_kernel(*refs, o_mode, n_chunks):
    if o_mode is None:
        h_ref, g_ref, wu_ref, wd_ref, out_ref = refs
        h = h_ref[...]
    else:
        h_ref, o_ref, wo_ref, g_ref, wu_ref, wd_ref, out_ref = refs
        o = o_ref[...]
        if o_mode == "transposed":
            proj = lax.dot_general(o, wo_ref[...], (((0,), (0,)), ((), ())), preferred_element_type=F32)
        else:
            proj = _dot(o, wo_ref[...])
        h = h_ref[...] + proj
    xn = _rms(h, g_ref[...]).astype(BF16)
    acc = h
    for c in range(n_chunks):
        cs = slice(c * FF_CHUNK, (c + 1) * FF_CHUNK)
        a = jnp.maximum(_dot(xn, wu_ref[:, cs]), 0.0)
        acc = acc + _dot((a * a).astype(BF16), wd_ref[cs, :])
    out_ref[...] = acc


def _mlp(h, g, w_up, w_down, o=None, w_o=None, o_transposed=False):
    N, D = h.shape
    F = w_up.shape[1]
    tm = TOK_BLOCK
    tok = pl.BlockSpec((tm, D), lambda i: (i, 0))
    in_specs = [tok]
    args = [h]
    o_mode = None
    if o is not None:
        if o_transposed:
            o_mode = "transposed"
            nb = o.shape[2] // tm
            in_specs.append(pl.BlockSpec((None, D, tm), lambda i: (i // nb, 0, i % nb)))
        else:
            o_mode = "rows"
            in_specs.append(tok)
        in_specs.append(_const_spec((D, D)))
        args += [o, w_o]
    in_specs += [_const_spec((1, D)), _const_spec((D, F)), _const_spec((F, D))]
    args += [g, w_up, w_down]
    return pl.pallas_call(
        functools.partial(_mlp_kernel, o_mode=o_mode, n_chunks=F // FF_CHUNK),
        grid=(N // tm,),
        in_specs=in_specs,
        out_specs=tok,
        out_shape=jax.ShapeDtypeStruct((N, D), F32),
        compiler_params=_params("arbitrary"),
        name="mlp" if o is None else "mlp_o",
    )(*args)


def _log_sigmoid(z):
    return -(jnp.maximum(-z, 0.0) + jnp.log1p(jnp.exp(-jnp.abs(z))))


def _kv_core(h_ref, g_ref, wk_ref, wv_ref, gk_ref, wf_ref, bf_ref, e_ref, et_ref, head_dim, v_transposed):
    hn = _rms(h_ref[...], g_ref[...]).astype(BF16)
    k = _head_rms(_dot(hn, wk_ref[...]), gk_ref[...], e_ref[...], et_ref[...], head_dim)
    v = _dot_nt(wv_ref[...], hn) if v_transposed else _dot(hn, wv_ref[...])
    lf = _log_sigmoid(_dot(hn, wf_ref[...]) + bf_ref[...])
    return k, v, lf


def _kv_sample_kernel(h_ref, g_ref, wk_ref, wv_ref, gk_ref, wf_ref, bf_ref, e_ref, et_ref,
                      k_ref, v_ref, lf_ref, kb_ref, vb_ref, *, head_dim):
    k, v, lf = _kv_core(h_ref, g_ref, wk_ref, wv_ref, gk_ref, wf_ref, bf_ref, e_ref, et_ref, head_dim, False)
    k_ref[...] = k
    v_ref[...] = v
    kb_ref[...] = k.astype(BF16)
    vb_ref[...] = v.astype(BF16)
    lf_ref[...] = lf[:, :N_HEADS]


def _kv_sample(h, g, wk, wv, gk_t, wf_p, bf_p, e, et):
    N, D = h.shape
    tm = TOK_BLOCK
    tok = pl.BlockSpec((tm, D), lambda i: (i, 0))
    return pl.pallas_call(
        functools.partial(_kv_sample_kernel, head_dim=D // N_HEADS),
        grid=(N // tm,),
        in_specs=[tok, _const_spec((1, D)), _const_spec((D, D)), _const_spec((D, D)), _const_spec((1, D)),
                  _const_spec((D, LANES)), _const_spec((1, LANES)), _const_spec((D, LANES)),
                  _const_spec((LANES, D))],
        out_specs=[tok, tok, pl.BlockSpec((tm, N_HEADS), lambda i: (i, 0)), tok, tok],
        out_shape=[jax.ShapeDtypeStruct((N, D), F32), jax.ShapeDtypeStruct((N, D), F32),
                   jax.ShapeDtypeStruct((N, N_HEADS), F32),
                   jax.ShapeDtypeStruct((N, D), BF16), jax.ShapeDtypeStruct((N, D), BF16)],
        compiler_params=_params("arbitrary"),
        name="shared_kv_sample",
    )(h, g, wk, wv, gk_t, wf_p, bf_p, e, et)


def _kv_prompt_kernel(h_ref, g_ref, wk_ref, wv_ref, gk_ref, wf_ref, bf_ref, e_ref, et_ref, tril_ref, spread_ref,
                      kt_ref, vt_ref, lft_ref, ka_ref, va_ref, carry_ref, *, nb, head_dim):
    i = pl.program_id(0)
    k, v_t, lf = _kv_core(h_ref, g_ref, wk_ref, wv_ref, gk_ref, wf_ref, bf_ref, e_ref, et_ref, head_dim, True)
    tm = k.shape[0]
    lane = lax.broadcasted_iota(jnp.int32, (tm, LANES), 1)
    lf = jnp.where(lane < N_HEADS, lf, 0.0)
    kt_ref[...] = k.T
    vt_ref[...] = v_t
    lft_ref[...] = lf.T[:N_HEADS]

    @pl.when(i % nb == 0)
    def _():
        carry_ref[...] = jnp.zeros_like(carry_ref)

    c = carry_ref[...]
    for part in _split3(lf):
        c = c + _dot(tril_ref[...], part)
    carry_ref[...] = c[tm - 1:, :]

    hi, mid, lo = (t.astype(F32) for t in _split3(c * -LOG2E))
    terms = hi + pltpu.roll(mid, N_HEADS, 1) + pltpu.roll(lo, 2 * N_HEADS, 1)
    aug = _dot(terms.astype(BF16), spread_ref[...])
    low = lane < head_dim
    for p in range(k.shape[1] // LANES):
        kp = k[:, p * LANES:(p + 1) * LANES]
        even = jnp.where(low, kp, 0.0) + aug[:, (2 * p) * LANES:(2 * p + 1) * LANES]
        odd = jnp.where(low, pltpu.roll(kp, head_dim, 1), 0.0) + aug[:, (2 * p + 1) * LANES:(2 * p + 2) * LANES]
        ka_ref[:, (2 * p) * LANES:(2 * p + 1) * LANES] = even.astype(BF16)
        ka_ref[:, (2 * p + 1) * LANES:(2 * p + 2) * LANES] = odd.astype(BF16)

    va_ref[:, 0:head_dim, :] = v_t.reshape(N_HEADS, head_dim, tm).astype(BF16)
    sub = lax.broadcasted_iota(jnp.int32, (N_HEADS, V_EXTRA, tm), 1)
    va_ref[:, head_dim:, :] = jnp.where(sub == 0, 1.0, 0.0).astype(BF16)


def _kv_prompt(h, T, g, wk, wv_t, gk_t, wf_p, bf_p, e, et):
    N, D = h.shape
    B = N // T
    head_dim = D // N_HEADS
    tm = TOK_BLOCK
    nb = T // tm
    tok = pl.BlockSpec((tm, D), lambda i: (i, 0))
    tril = (jnp.arange(tm)[:, None] >= jnp.arange(tm)[None, :]).astype(BF16)
    src = jnp.arange(LANES)[:, None]
    dst = jnp.arange(N_HEADS * LANES)[None, :]
    spread = ((src < 3 * N_HEADS) & (dst // LANES == src % N_HEADS)
              & (dst % LANES == head_dim + src // N_HEADS)).astype(BF16)
    seq_t = lambda r: pl.BlockSpec((None, r, tm), lambda i: (i // nb, 0, i % nb))
    return pl.pallas_call(
        functools.partial(_kv_prompt_kernel, nb=nb, head_dim=head_dim),
        grid=(N // tm,),
        in_specs=[tok, _const_spec((1, D)), _const_spec((D, D)), _const_spec((D, D)), _const_spec((1, D)),
                  _const_spec((D, LANES)), _const_spec((1, LANES)), _const_spec((D, LANES)),
                  _const_spec((LANES, D)), _const_spec((tm, tm)), _const_spec((LANES, N_HEADS * LANES))],
        out_specs=[seq_t(D), seq_t(D), seq_t(N_HEADS),
                   pl.BlockSpec((tm, N_HEADS * LANES), lambda i: (i, 0)),
                   pl.BlockSpec((None, N_HEADS, head_dim + V_EXTRA, tm), lambda i: (i // nb, 0, 0, i % nb))],
        out_shape=[jax.ShapeDtypeStruct((B, D, T), F32), jax.ShapeDtypeStruct((B, D, T), F32),
                   jax.ShapeDtypeStruct((B, N_HEADS, T), F32),
                   jax.ShapeDtypeStruct((N, N_HEADS * LANES), BF16),
                   jax.ShapeDtypeStruct((B, N_HEADS, head_dim + V_EXTRA, T), BF16)],
        scratch_shapes=[pltpu.VMEM((1, LANES), F32)],
        compiler_params=_params("arbitrary"),
        name="shared_kv_prompt",
    )(h, g, wk, wv_t, gk_t, wf_p, bf_p, e, et, tril, spread)


def _q_kernel(h_ref, g_ref, wq_ref, gq_ref, e_ref, et_ref, q_ref, *, head_dim, transposed):
    xn = _rms(h_ref[...], g_ref[...]).astype(BF16)
    q = _head_rms(_dot(xn, wq_ref[...]), gq_ref[...], e_ref[...], et_ref[...], head_dim)
    if not transposed:
        q_ref[...] = (q * (head_dim ** -0.5)).astype(BF16)
        return
    tm = q.shape[0]
    q_t = (q * (head_dim ** -0.5 * LOG2E)).T.reshape(N_HEADS, head_dim, tm)
    q_ref[:, 0:head_dim, :] = q_t.astype(BF16)
    sub = lax.broadcasted_iota(jnp.int32, (N_HEADS, LANES - head_dim, tm), 1)
    q_ref[:, head_dim:, :] = jnp.where(sub < 3, 1.0, 0.0).astype(BF16)


def _qproj(h, g, wq, gq_t, e, et, T=None):
    N, D = h.shape
    tm = TOK_BLOCK
    tok = pl.BlockSpec((tm, D), lambda i: (i, 0))
    if T is None:
        out_spec, out_shape = tok, jax.ShapeDtypeStruct((N, D), BF16)
    else:
        nb = T // tm
        out_spec = pl.BlockSpec((None, N_HEADS, LANES, tm), lambda i: (i // nb, 0, 0, i % nb))
        out_shape = jax.ShapeDtypeStruct((N // T, N_HEADS, LANES, T), BF16)
    return pl.pallas_call(
        functools.partial(_q_kernel, head_dim=D // N_HEADS, transposed=T is not None),
        grid=(N // tm,),
        in_specs=[tok, _const_spec((1, D)), _const_spec((D, D)), _const_spec((1, D)),
                  _const_spec((D, LANES)), _const_spec((LANES, D))],
        out_specs=out_spec,
        out_shape=out_shape,
        compiler_params=_params("arbitrary"),
        name="q_proj" if T is None else "q_proj_t",
    )(h, g, wq, gq_t, e, et)


def _attn_prompt_kernel(q_ref, k_ref, v_ref, o_ref, *, blk, head_dim):
    qi = pl.program_id(2)
    n_h = q_ref.shape[0]
    q_t = [q_ref[x] for x in range(n_h)]
    key_iota = lax.broadcasted_iota(jnp.int32, (blk, blk), 0)
    qry_iota = lax.broadcasted_iota(jnp.int32, (blk, blk), 1)

    def step(j, carry, diagonal):
        off = pl.multiple_of(j * blk, blk)
        s_all = [_dot(k_ref[pl.ds(off, blk), x * LANES:(x + 1) * LANES], q_t[x]) for x in range(n_h)]
        out = []
        for x, (m, acc) in enumerate(carry):
            s = s_all[x]
            if diagonal:
                s = jnp.where(key_iota <= qry_iota, s, NEG_BIG)
            m_new = jnp.maximum(m, jnp.max(s, axis=0, keepdims=True))
            p = jnp.exp2(s - m_new).astype(BF16)
            out.append((m_new, jnp.exp2(m - m_new) * acc + _dot(v_ref[x, :, pl.ds(off, blk)], p)))
        return tuple(out)

    init = tuple((jnp.full((1, blk), NEG_BIG, F32), jnp.zeros((v_ref.shape[1], blk), F32)) for _ in range(n_h))
    carry = lax.fori_loop(0, qi, lambda j, c: step(j, c, False), init)
    for x, (_, acc) in enumerate(step(qi, carry, True)):
        o_ref[x] = (acc[:head_dim] / acc[head_dim:head_dim + 1]).astype(o_ref.dtype)


def _attn_prompt(q_t, k_aug, v_aug):
    B, H, _, T = q_t.shape
    vr = v_aug.shape[2]
    head_dim = vr - V_EXTRA
    blk = ATT_BLOCK
    hb = ATT_HEADS
    return pl.pallas_call(
        functools.partial(_attn_prompt_kernel, blk=blk, head_dim=head_dim),
        grid=(B, H // hb, T // blk),
        in_specs=[
            pl.BlockSpec((None, hb, LANES, blk), lambda b, h, i: (b, h, 0, i)),
            pl.BlockSpec((None, T, hb * LANES), lambda b, h, i: (b, 0, h)),
            pl.BlockSpec((None, hb, vr, T), lambda b, h, i: (b, h, 0, 0)),
        ],
        out_specs=pl.BlockSpec((None, hb, head_dim, blk), lambda b, h, i: (b, h, 0, i)),
        out_shape=jax.ShapeDtypeStruct((B, H, head_dim, T), BF16),
        compiler_params=_params("arbitrary", "arbitrary", "arbitrary"),
        name="attn_prompt",
    )(q_t, k_aug, v_aug)


def _attn_sample_kernel(pt_ref, q_ref, lfn_ref, kn_ref, vn_ref, *refs, n_pages, page, ts, head_dim, new_pad):
    del pt_ref
    lf_refs = refs[:n_pages]
    k_refs = refs[n_pages:2 * n_pages]
    v_refs = refs[2 * n_pages:3 * n_pages]
    o_ref = refs[3 * n_pages]
    D = q_ref.shape[1]
    rows = LANES
    reps = rows // N_HEADS
    hrow = lax.broadcasted_iota(jnp.int32, (N_HEADS, D), 0)
    hcol = lax.broadcasted_iota(jnp.int32, (N_HEADS, D), 1) // head_dim
    head_mask = hrow == hcol
    q = q_ref[...].astype(F32)
    zq = jnp.zeros((N_HEADS, D), F32)
    qbd = jnp.concatenate(
        [jnp.where(head_mask, jnp.broadcast_to(q[t:t + 1, :], (N_HEADS, D)), zq) for t in range(ts)]
        + [zq] * (reps - ts), axis=0).astype(BF16)
    tri = (lax.broadcasted_iota(jnp.int32, (page, page), 0)
           <= lax.broadcasted_iota(jnp.int32, (page, page), 1)).astype(BF16)

    def running_sum(lf_t, carry):
        parts = _dot(jnp.concatenate(_split3(lf_t), axis=0), tri)
        return parts[:N_HEADS] + parts[N_HEADS:2 * N_HEADS] + parts[2 * N_HEADS:] + carry

    def update(state, s, pv_fn):
        m, l, acc = state
        m_new = jnp.maximum(m, jnp.max(s, axis=1, keepdims=True))
        alpha = jnp.exp(m - m_new)
        p = jnp.exp(s - m_new)
        l = alpha * l + jnp.sum(p, axis=1, keepdims=True)
        acc = alpha * acc + pv_fn(p.astype(BF16))
        return m_new, l, acc

    def sub_c(s, c):
        n = s.shape[1]
        return (s.reshape(reps, N_HEADS, n) - c[None]).reshape(rows, n)

    state = (jnp.full((rows, 1), NEG_BIG, F32), jnp.zeros((rows, 1), F32), jnp.zeros((rows, D), F32))
    carry = jnp.zeros((N_HEADS, 1), F32)
    for j in range(n_pages):
        c = running_sum(lf_refs[j][...], carry)
        carry = c[:, page - 1:]
        k_t = k_refs[j][...].astype(BF16)
        v_t = v_refs[j][...].astype(BF16)
        state = update(state, sub_c(_dot(qbd, k_t), c), lambda p, v_t=v_t: _dot_nt(p, v_t))
    c = running_sum(lfn_ref[...], carry)[:, :new_pad]
    s = sub_c(_dot_nt(qbd, kn_ref[...]), c)
    t_row = lax.broadcasted_iota(jnp.int32, (rows, new_pad), 0) // N_HEADS
    t_key = lax.broadcasted_iota(jnp.int32, (rows, new_pad), 1)
    s = jnp.where(t_key <= t_row, s, NEG_BIG)
    vn = vn_ref[...]
    _, l, acc = update(state, s, lambda p: _dot(p, vn))
    o = acc / l
    zo = jnp.zeros((N_HEADS, D), F32)
    for t in range(ts):
        blk = jnp.where(head_mask, o[t * N_HEADS:(t + 1) * N_HEADS, :], zo)
        o_ref[t:t + 1, :] = jnp.sum(blk, axis=0, keepdims=True).astype(o_ref.dtype)


def _attn_sample(q, lf_new_t, k_new, v_new, cache_k, cache_v, cache_logf, page_table):
    bs, ts, D = q.shape
    n_pages = page_table.shape[1]
    n_pool, page = cache_k.shape[0], cache_k.shape[1]
    assert page == LANES
    new_pad = k_new.shape[1]
    ck = jnp.transpose(cache_k, (0, 2, 3, 1)).reshape(n_pool, D, page)
    cv = jnp.transpose(cache_v, (0, 2, 3, 1)).reshape(n_pool, D, page)
    cl = jnp.transpose(cache_logf, (0, 2, 1))
    seq = lambda n: pl.BlockSpec((None, n, D), lambda b, pt: (b, 0, 0))
    paged = lambda r: [pl.BlockSpec((None, r, page), lambda b, pt, j=j: (pt[b, j], 0, 0)) for j in range(n_pages)]
    return pl.pallas_call(
        functools.partial(_attn_sample_kernel, n_pages=n_pages, page=page, ts=ts,
                          head_dim=D // N_HEADS, new_pad=new_pad),
        grid_spec=pltpu.PrefetchScalarGridSpec(
            num_scalar_prefetch=1,
            grid=(bs,),
            in_specs=[seq(ts), pl.BlockSpec((None, N_HEADS, page), lambda b, pt: (b, 0, 0)),
                      seq(new_pad), seq(new_pad)] + paged(N_HEADS) + paged(D) + paged(D),
            out_specs=seq(ts),
        ),
        out_shape=jax.ShapeDtypeStruct((bs, ts, D), BF16),
        compiler_params=_params("arbitrary"),
        name="attn_sample",
    )(page_table, q, lf_new_t, k_new, v_new, *([cl] * n_pages), *([ck] * n_pages), *([cv] * n_pages))


def kernel(x_prompt, x_sample, state_pool, cache_k, cache_v, cache_logf, page_table, g_pool, w_pool, pool_scale,
           g_attn, w_q, g_q, w_o, g_kv, w_k, w_v, g_k, w_f, b_f, g_mlp, w_up, w_down):
    Bp, Tp, D = x_prompt.shape
    Bs, Ts, _ = x_sample.shape
    n_a = g_pool.shape[0]
    n_b = g_attn.shape[0]
    head_dim = D // N_HEADS
    past = page_table.shape[1] * cache_k.shape[1]
    new_pad = 16

    row = lambda a: a.reshape(1, -1).astype(F32)
    w_pool_b = w_pool.astype(BF16)
    w_up_b, w_down_b = w_up.astype(BF16), w_down.astype(BF16)
    w_q_b, w_o_b = w_q.astype(BF16), w_o.astype(BF16)
    w_k_b, w_v_b = w_k.astype(BF16), w_v.astype(BF16)
    w_f_p = jnp.pad(w_f, ((0, 0), (0, LANES - N_HEADS))).astype(BF16)
    b_f_p = jnp.pad(b_f.astype(F32), (0, LANES - N_HEADS)).reshape(1, LANES)
    e = (jnp.arange(D)[:, None] // head_dim == jnp.arange(LANES)[None, :]).astype(BF16)
    et = e.T
    g_k_t = row(jnp.tile(g_k, N_HEADS))

    h_p = x_prompt
    h_s = jnp.transpose(x_sample, (1, 0, 2))
    st_p, st_s = [], []
    for l in range(n_a):
        args = (row(g_pool[l]), w_pool_b[l], row(pool_scale[l]))
        h_p, tail = _pool_prompt(h_p, *args)
        st_p.append(tail[:, HALO - POOL_BUF:])
        h_p = _mlp(h_p.reshape(Bp * Tp, D), row(g_mlp[l]), w_up_b[l], w_down_b[l]).reshape(Bp, Tp, D)
        h_s, st = _pool_sample(h_s, jnp.transpose(state_pool[l], (1, 0, 2)), *args, past)
        st_s.append(jnp.transpose(st, (1, 0, 2)))
        h_s = _mlp(h_s.reshape(Ts * Bs, D), row(g_mlp[l]), w_up_b[l], w_down_b[l]).reshape(Ts, Bs, D)
    pool_state_prompt = jnp.stack(st_p, axis=0)
    pool_state_sample = jnp.stack(st_s, axis=0)

    h_p = h_p.reshape(Bp * Tp, D)
    h_s = jnp.transpose(h_s, (1, 0, 2)).reshape(Bs * Ts, D)

    kv_tail = (g_k_t, w_f_p, b_f_p, e, et)
    kt_p, vt_p, lft_p, ka_p, va_p = _kv_prompt(h_p, Tp, row(g_kv), w_k_b, w_v_b.T, *kv_tail)
    k_s, v_s, lf_s, kb_s, vb_s = _kv_sample(h_s, row(g_kv), w_k_b, w_v_b, *kv_tail)
    pad_new = lambda a, n: jnp.pad(a.reshape(Bs, Ts, -1), ((0, 0), (0, n - Ts), (0, 0)))
    lfn_t = jnp.transpose(pad_new(lf_s, cache_logf.shape[1]), (0, 2, 1))
    kn, vn = pad_new(kb_s, new_pad), pad_new(vb_s, new_pad)
    ka_p = ka_p.reshape(Bp, Tp, N_HEADS * LANES)

    for l in range(n_b):
        j = n_a + l
        g_q_t = row(jnp.tile(g_q[l], N_HEADS))
        q_p = _qproj(h_p, row(g_attn[l]), w_q_b[l], g_q_t, e, et, T=Tp)
        o_p = _attn_prompt(q_p, ka_p, va_p).reshape(Bp, D, Tp)
        h_p = _mlp(h_p, row(g_mlp[j]), w_up_b[j], w_down_b[j], o_p, w_o_b[l], o_transposed=True)
        q_s = _qproj(h_s, row(g_attn[l]), w_q_b[l], g_q_t, e, et)
        o_s = _attn_sample(q_s.reshape(Bs, Ts, D), lfn_t, kn, vn, cache_k, cache_v, cache_logf, page_table)
        h_s = _mlp(h_s, row(g_mlp[j]), w_up_b[j], w_down_b[j], o_s.reshape(Bs * Ts, D), w_o_b[l])

    hd_t = lambda a: jnp.transpose(a.reshape(Bp, N_HEADS, head_dim, Tp), (0, 3, 1, 2))
    hd4 = lambda a: a.reshape(Bs, Ts, N_HEADS, head_dim)
    return (h_p.reshape(Bp, Tp, D), h_s.reshape(Bs, Ts, D), pool_state_prompt, pool_state_sample,
            hd_t(kt_p), hd_t(vt_p), jnp.transpose(lft_p, (0, 2, 1)).astype(cache_logf.dtype),
            hd4(k_s), hd4(v_s), lf_s.reshape(Bs, Ts, N_HEADS).astype(cache_logf.dtype))
```

```python
import functools

import jax
import jax.numpy as jnp
from jax import lax
from jax.experimental import pallas as pl
from jax.experimental.pallas import tpu as pltpu

F32 = jnp.float32
BF16 = jnp.bfloat16

EPS = 1e-6
POOL_WINDOWS = (2, 4, 8, 16)
POOL_BUF = max(POOL_WINDOWS) - 1
N_HEADS = 16
LANES = 128
HALO = 16
VMEM_LIMIT = 56 * 1024 * 1024
NEG_BIG = -1e30

TOK_BLOCK = 512
FF_CHUNK = 1024
ATT_BLOCK = 512
ATT_HEADS = 4
V_EXTRA = 16
LOG2E = 1.4426950408889634
MAX_SELF_DEFICIT = 120.0


def _dot(a, b):
    return jnp.dot(a, b, preferred_element_type=F32)


def _dot_nt(a, b):
    return lax.dot_general(a, b, (((1,), (1,)), ((), ())), preferred_element_type=F32)


def _rms(x, g):
    ms = jnp.mean(x * x, axis=-1, keepdims=True)
    return x * lax.rsqrt(ms + EPS) * g


def _split2(x):
    hi = x.astype(BF16)
    lo = (x - hi.astype(F32)).astype(BF16)
    return hi, lo


def _split3(x):
    hi = x.astype(BF16)
    r1 = x - hi.astype(F32)
    mid = r1.astype(BF16)
    lo = (r1 - mid.astype(F32)).astype(BF16)
    return hi, mid, lo


def _head_rms(x, g_tiled, e, et, head_dim):
    hi, lo = _split2(x * x)
    ss = _dot(hi, e) + _dot(lo, e)
    r = lax.rsqrt(ss * (1.0 / head_dim) + EPS)
    rh, rl = _split2(r)
    return x * (_dot(rh, et) + _dot(rl, et)) * g_tiled


def _const_spec(shape):
    return pl.BlockSpec(shape, lambda *_: (0,) * len(shape), pipeline_mode=pl.Buffered(1))


def _params(*sem):
    return pltpu.CompilerParams(dimension_semantics=sem, vmem_limit_bytes=VMEM_LIMIT)


def _pool_prompt_kernel(x_ref, halo_ref, g_ref, w_ref, sc_ref, o_ref, st_ref, ext_ref, *, tb, gdim):
    i = pl.program_id(1)
    g = g_ref[...]
    x = x_ref[...]
    xn = _rms(x, g)
    hn = _rms(halo_ref[...], g)
    ext_ref[0:HALO, :] = jnp.where(i > 0, hn, 0.0)
    ext_ref[HALO:, :] = xn
    pos = i * tb + lax.broadcasted_iota(jnp.int32, (tb, 1), 0)
    sc = sc_ref[...]
    for gi, w in enumerate(POOL_WINDOWS):
        sl = slice(gi * gdim, (gi + 1) * gdim)
        acc = xn[:, sl]
        for j in range(1, w):
            acc = acc + ext_ref[HALO - j:HALO - j + tb, sl]
        cnt = jnp.minimum(pos + 1, w).astype(F32)
        pooled = acc / cnt - xn[:, sl]
        mixed = _dot(pooled.astype(BF16), w_ref[gi]) * sc[:, sl]
        o_ref[:, sl] = x[:, sl] + mixed
    st_ref[...] = xn[tb - HALO:, :]


def _pool_prompt(h, g, w, sc):
    B, T, D = h.shape
    tb = TOK_BLOCK
    gdim = D // len(POOL_WINDOWS)
    r = tb // HALO
    return pl.pallas_call(
        functools.partial(_pool_prompt_kernel, tb=tb, gdim=gdim),
        grid=(B, T // tb),
        in_specs=[
            pl.BlockSpec((None, tb, D), lambda b, i: (b, i, 0)),
            pl.BlockSpec((None, HALO, D), lambda b, i: (b, jnp.maximum(i * r - 1, 0), 0)),
            _const_spec((1, D)),
            _const_spec((len(POOL_WINDOWS), gdim, gdim)),
            _const_spec((1, D)),
        ],
        out_specs=[
            pl.BlockSpec((None, tb, D), lambda b, i: (b, i, 0)),
            pl.BlockSpec((None, HALO, D), lambda b, i: (b, 0, 0)),
        ],
        out_shape=[jax.ShapeDtypeStruct((B, T, D), F32), jax.ShapeDtypeStruct((B, HALO, D), F32)],
        scratch_shapes=[pltpu.VMEM((tb + HALO, D), F32)],
        compiler_params=_params("arbitrary", "arbitrary"),
        name="pool_prompt",
    )(h, h, g, w, sc)


def _pool_sample_kernel(x_ref, pf_ref, g_ref, w_ref, sc_ref, o_ref, st_ref, *, ts, bb, gdim, cnts):
    g = g_ref[...]
    sc = sc_ref[...]
    xs = [x_ref[t] for t in range(ts)]
    xn = [_rms(x, g) for x in xs]
    ext = [pf_ref[r] for r in range(POOL_BUF)] + xn
    for gi, w in enumerate(POOL_WINDOWS):
        sl = slice(gi * gdim, (gi + 1) * gdim)
        pooled = []
        for t in range(ts):
            acc = ext[POOL_BUF + t][:, sl]
            for j in range(1, w):
                acc = acc + ext[POOL_BUF + t - j][:, sl]
            pooled.append(acc / cnts[t][gi] - xn[t][:, sl])
        mixed = _dot(jnp.concatenate(pooled, axis=0).astype(BF16), w_ref[gi]) * sc[:, sl]
        for t in range(ts):
            o_ref[t, :, sl] = xs[t][:, sl] + mixed[t * bb:(t + 1) * bb]
    for r in range(POOL_BUF):
        st_ref[r] = ext[ts + r]


def _pool_sample(h_tm, prefix_tm, g, w, sc, past):
    ts, bs, D = h_tm.shape
    bb = 32
    gdim = D // len(POOL_WINDOWS)
    cnts = tuple(tuple(float(min(past + t + 1, w)) for w in POOL_WINDOWS) for t in range(ts))
    return pl.pallas_call(
        functools.partial(_pool_sample_kernel, ts=ts, bb=bb, gdim=gdim, cnts=cnts),
        grid=(bs // bb,),
        in_specs=[
            pl.BlockSpec((ts, bb, D), lambda i: (0, i, 0)),
            pl.BlockSpec((POOL_BUF, bb, D), lambda i: (0, i, 0)),
            _const_spec((1, D)),
            _const_spec((len(POOL_WINDOWS), gdim, gdim)),
            _const_spec((1, D)),
        ],
        out_specs=[
            pl.BlockSpec((ts, bb, D), lambda i: (0, i, 0)),
            pl.BlockSpec((POOL_BUF, bb, D), lambda i: (0, i, 0)),
        ],
        out_shape=[jax.ShapeDtypeStruct((ts, bs, D), F32), jax.ShapeDtypeStruct((POOL_BUF, bs, D), F32)],
        compiler_params=_params("arbitrary"),
        name="pool_sample",
    )(h_tm, prefix_tm, g, w, sc)


def _mlp_kernel(*refs, o_mode, n_chunks):
    if o_mode is None:
        h_ref, g_ref, wu_ref, wd_ref, out_ref = refs
        h = h_ref[...]
    else:
        h_ref, o_ref, wo_ref, g_ref, wu_ref, wd_ref, out_ref = refs
        o = o_ref[...]
        if o_mode == "transposed":
            proj = lax.dot_general(o, wo_ref[...], (((0,), (0,)), ((), ())), preferred_element_type=F32)
        else:
            proj = _dot(o, wo_ref[...])
        h = h_ref[...] + proj
    xn = _rms(h, g_ref[...]).astype(BF16)
    acc = h
    for c in range(n_chunks):
        cs = slice(c * FF_CHUNK, (c + 1) * FF_CHUNK)
        a = jnp.maximum(_dot(xn, wu_ref[:, cs]), 0.0)
        acc = acc + _dot((a * a).astype(BF16), wd_ref[cs, :])
    out_ref[...] = acc


def _mlp(h, g, w_up, w_down, o=None, w_o=None, o_transposed=False):
    N, D = h.shape
    F = w_up.shape[1]
    tm = TOK_BLOCK
    tok = pl.BlockSpec((tm, D), lambda i: (i, 0))
    in_specs = [tok]
    args = [h]
    o_mode = None
    if o is not None:
        if o_transposed:
            o_mode = "transposed"
            nb = o.shape[2] // tm
            in_specs.append(pl.BlockSpec((None, D, tm), lambda i: (i // nb, 0, i % nb)))
        else:
            o_mode = "rows"
            in_specs.append(tok)
        in_specs.append(_const_spec((D, D)))
        args += [o, w_o]
    in_specs += [_const_spec((1, D)), _const_spec((D, F)), _const_spec((F, D))]
    args += [g, w_up, w_down]
    return pl.pallas_call(
        functools.partial(_mlp_kernel, o_mode=o_mode, n_chunks=F // FF_CHUNK),
        grid=(N // tm,),
        in_specs=in_specs,
        out_specs=tok,
        out_shape=jax.ShapeDtypeStruct((N, D), F32),
        compiler_params=_params("arbitrary"),
        name="mlp" if o is None else "mlp_o",
    )(*args)


def _log_sigmoid(z):
    return -(jnp.maximum(-z, 0.0) + jnp.log1p(jnp.exp(-jnp.abs(z))))


def _kv_core(h_ref, g_ref, wk_ref, wv_ref, gk_ref, wf_ref, bf_ref, e_ref, et_ref, head_dim, v_transposed):
    hn = _rms(h_ref[...], g_ref[...]).astype(BF16)
    k = _head_rms(_dot(hn, wk_ref[...]), gk_ref[...], e_ref[...], et_ref[...], head_dim)
    v = _dot_nt(wv_ref[...], hn) if v_transposed else _dot(hn, wv_ref[...])
    lf = _log_sigmoid(_dot(hn, wf_ref[...]) + bf_ref[...])
    return k, v, lf


def _kv_sample_kernel(h_ref, g_ref, wk_ref, wv_ref, gk_ref, wf_ref, bf_ref, e_ref, et_ref,
                      k_ref, v_ref, lf_ref, kb_ref, vb_ref, *, head_dim):
    k, v, lf = _kv_core(h_ref, g_ref, wk_ref, wv_ref, gk_ref, wf_ref, bf_ref, e_ref, et_ref, head_dim, False)
    k_ref[...] = k
    v_ref[...] = v
    kb_ref[...] = k.astype(BF16)
    vb_ref[...] = v.astype(BF16)
    lf_ref[...] = lf[:, :N_HEADS]


def _kv_sample(h, g, wk, wv, gk_t, wf_p, bf_p, e, et):
    N, D = h.shape
    tm = TOK_BLOCK
    tok = pl.BlockSpec((tm, D), lambda i: (i, 0))
    return pl.pallas_call(
        functools.partial(_kv_sample_kernel, head_dim=D // N_HEADS),
        grid=(N // tm,),
        in_specs=[tok, _const_spec((1, D)), _const_spec((D, D)), _const_spec((D, D)), _const_spec((1, D)),
                  _const_spec((D, LANES)), _const_spec((1, LANES)), _const_spec((D, LANES)),
                  _const_spec((LANES, D))],
        out_specs=[tok, tok, pl.BlockSpec((tm, N_HEADS), lambda i: (i, 0)), tok, tok],
        out_shape=[jax.ShapeDtypeStruct((N, D), F32), jax.ShapeDtypeStruct((N, D), F32),
                   jax.ShapeDtypeStruct((N, N_HEADS), F32),
                   jax.ShapeDtypeStruct((N, D), BF16), jax.ShapeDtypeStruct((N, D), BF16)],
        compiler_params=_params("arbitrary"),
        name="shared_kv_sample",
    )(h, g, wk, wv, gk_t, wf_p, bf_p, e, et)


def _kv_prompt_kernel(h_ref, g_ref, wk_ref, wv_ref, gk_ref, wf_ref, bf_ref, e_ref, et_ref, tril_ref, spread_ref,
                      kt_ref, vt_ref, lft_ref, ct_ref, ka_ref, va_ref, carry_ref, *, nb, head_dim):
    i = pl.program_id(0)
    k, v_t, lf = _kv_core(h_ref, g_ref, wk_ref, wv_ref, gk_ref, wf_ref, bf_ref, e_ref, et_ref, head_dim, True)
    tm = k.shape[0]
    lane = lax.broadcasted_iota(jnp.int32, (tm, LANES), 1)
    lf = jnp.where(lane < N_HEADS, lf, 0.0)
    kt_ref[...] = k.T
    vt_ref[...] = v_t
    lft_ref[...] = lf.T[:N_HEADS]

    @pl.when(i % nb == 0)
    def _():
        carry_ref[...] = jnp.zeros_like(carry_ref)

    c = carry_ref[...]
    for part in _split3(lf):
        c = c + _dot(tril_ref[...], part)
    carry_ref[...] = c[tm - 1:, :]
    ct_ref[...] = c.T[:N_HEADS]

    hi, mid, lo = (t.astype(F32) for t in _split3(c * -LOG2E))
    terms = hi + pltpu.roll(mid, N_HEADS, 1) + pltpu.roll(lo, 2 * N_HEADS, 1)
    ones = jnp.where((lane >= head_dim + 3) & (lane < head_dim + 6), 1.0, 0.0)
    aug = _dot(terms.astype(BF16), spread_ref[...])
    low = lane < head_dim
    for p in range(k.shape[1] // LANES):
        kp = k[:, p * LANES:(p + 1) * LANES]
        even = jnp.where(low, kp, ones) + aug[:, (2 * p) * LANES:(2 * p + 1) * LANES]
        odd = jnp.where(low, pltpu.roll(kp, head_dim, 1), ones) + aug[:, (2 * p + 1) * LANES:(2 * p + 2) * LANES]
        ka_ref[:, (2 * p) * LANES:(2 * p + 1) * LANES] = even.astype(BF16)
        ka_ref[:, (2 * p + 1) * LANES:(2 * p + 2) * LANES] = odd.astype(BF16)

    va_ref[:, 0:head_dim, :] = v_t.reshape(N_HEADS, head_dim, tm).astype(BF16)
    sub = lax.broadcasted_iota(jnp.int32, (N_HEADS, V_EXTRA, tm), 1)
    va_ref[:, head_dim:, :] = jnp.where(sub == 0, 1.0, 0.0).astype(BF16)


def _kv_prompt(h, T, g, wk, wv_t, gk_t, wf_p, bf_p, e, et):
    N, D = h.shape
    B = N // T
    head_dim = D // N_HEADS
    tm = TOK_BLOCK
    nb = T // tm
    tok = pl.BlockSpec((tm, D), lambda i: (i, 0))
    tril = (jnp.arange(tm)[:, None] >= jnp.arange(tm)[None, :]).astype(BF16)
    src = jnp.arange(LANES)[:, None]
    dst = jnp.arange(N_HEADS * LANES)[None, :]
    spread = ((src < 3 * N_HEADS) & (dst // LANES == src % N_HEADS)
              & (dst % LANES == head_dim + src // N_HEADS)).astype(BF16)
    seq_t = lambda r: pl.BlockSpec((None, r, tm), lambda i: (i // nb, 0, i % nb))
    return pl.pallas_call(
        functools.partial(_kv_prompt_kernel, nb=nb, head_dim=head_dim),
        grid=(N // tm,),
        in_specs=[tok, _const_spec((1, D)), _const_spec((D, D)), _const_spec((D, D)), _const_spec((1, D)),
                  _const_spec((D, LANES)), _const_spec((1, LANES)), _const_spec((D, LANES)),
                  _const_spec((LANES, D)), _const_spec((tm, tm)), _const_spec((LANES, N_HEADS * LANES))],
        out_specs=[seq_t(D), seq_t(D), seq_t(N_HEADS), seq_t(N_HEADS),
                   pl.BlockSpec((tm, N_HEADS * LANES), lambda i: (i, 0)),
                   pl.BlockSpec((None, N_HEADS, head_dim + V_EXTRA, tm), lambda i: (i // nb, 0, 0, i % nb))],
        out_shape=[jax.ShapeDtypeStruct((B, D, T), F32), jax.ShapeDtypeStruct((B, D, T), F32),
                   jax.ShapeDtypeStruct((B, N_HEADS, T), F32), jax.ShapeDtypeStruct((B, N_HEADS, T), F32),
                   jax.ShapeDtypeStruct((N, N_HEADS * LANES), BF16),
                   jax.ShapeDtypeStruct((B, N_HEADS, head_dim + V_EXTRA, T), BF16)],
        scratch_shapes=[pltpu.VMEM((1, LANES), F32)],
        compiler_params=_params("arbitrary"),
        name="shared_kv_prompt",
    )(h, g, wk, wv_t, gk_t, wf_p, bf_p, e, et, tril, spread)


def _q_kernel(*refs, head_dim, transposed):
    if transposed:
        h_ref, g_ref, wq_ref, gq_ref, e_ref, et_ref, ct_ref, bound_ref, q_ref = refs
    else:
        h_ref, g_ref, wq_ref, gq_ref, e_ref, et_ref, q_ref = refs
    xn = _rms(h_ref[...], g_ref[...]).astype(BF16)
    q = _head_rms(_dot(xn, wq_ref[...]), gq_ref[...], e_ref[...], et_ref[...], head_dim)
    if not transposed:
        q_ref[...] = (q * (head_dim ** -0.5)).astype(BF16)
        return
    tm = q.shape[0]
    q_t = (q * (head_dim ** -0.5 * LOG2E)).T.reshape(N_HEADS, head_dim, tm)
    q_ref[:, 0:head_dim, :] = q_t.astype(BF16)
    shape = (N_HEADS, LANES - head_dim, tm)
    sub = lax.broadcasted_iota(jnp.int32, shape, 1)
    hi, mid, lo = (jnp.broadcast_to(t.astype(F32)[:, None, :], shape)
                   for t in _split3(ct_ref[...] * LOG2E - bound_ref[...]))
    tail = jnp.where(sub < 3, 1.0, jnp.where(sub == 3, hi, jnp.where(sub == 4, mid, jnp.where(sub == 5, lo, 0.0))))
    q_ref[:, head_dim:, :] = tail.astype(BF16)


def _qproj(h, g, wq, gq_t, e, et, ct=None, bound=None):
    N, D = h.shape
    tm = TOK_BLOCK
    tok = pl.BlockSpec((tm, D), lambda i: (i, 0))
    in_specs = [tok, _const_spec((1, D)), _const_spec((D, D)), _const_spec((1, D)),
                _const_spec((D, LANES)), _const_spec((LANES, D))]
    args = [h, g, wq, gq_t, e, et]
    if ct is None:
        out_spec, out_shape = tok, jax.ShapeDtypeStruct((N, D), BF16)
    else:
        T = ct.shape[2]
        nb = T // tm
        in_specs += [pl.BlockSpec((None, N_HEADS, tm), lambda i: (i // nb, 0, i % nb)), _const_spec((1, 1))]
        args += [ct, bound]
        out_spec = pl.BlockSpec((None, N_HEADS, LANES, tm), lambda i: (i // nb, 0, 0, i % nb))
        out_shape = jax.ShapeDtypeStruct((N // T, N_HEADS, LANES, T), BF16)
    return pl.pallas_call(
        functools.partial(_q_kernel, head_dim=D // N_HEADS, transposed=ct is not None),
        grid=(N // tm,),
        in_specs=in_specs,
        out_specs=out_spec,
        out_shape=out_shape,
        compiler_params=_params("arbitrary"),
        name="q_proj" if ct is None else "q_proj_t",
    )(*args)


def _attn_prompt_kernel(q_ref, k_ref, v_ref, o_ref, *, blk, head_dim, bounded):
    qi = pl.program_id(2)
    n_h = q_ref.shape[0]
    q_t = [q_ref[x] for x in range(n_h)]
    key_iota = lax.broadcasted_iota(jnp.int32, (blk, blk), 0)
    qry_iota = lax.broadcasted_iota(jnp.int32, (blk, blk), 1)

    def step(j, carry, diagonal):
        off = pl.multiple_of(j * blk, blk)
        s_all = [_dot(k_ref[pl.ds(off, blk), x * LANES:(x + 1) * LANES], q_t[x]) for x in range(n_h)]
        out = []
        for x, (m, acc) in enumerate(carry):
            s = s_all[x]
            if diagonal:
                s = jnp.where(key_iota <= qry_iota, s, NEG_BIG)
            if bounded:
                out.append((m, acc + _dot(v_ref[x, :, pl.ds(off, blk)], jnp.exp2(s).astype(BF16))))
                continue
            m_new = jnp.maximum(m, jnp.max(s, axis=0, keepdims=True))
            p = jnp.exp2(s - m_new).astype(BF16)
            out.append((m_new, jnp.exp2(m - m_new) * acc + _dot(v_ref[x, :, pl.ds(off, blk)], p)))
        return tuple(out)

    init = tuple((jnp.full((1, blk), NEG_BIG, F32), jnp.zeros((v_ref.shape[1], blk), F32)) for _ in range(n_h))
    carry = lax.fori_loop(0, qi, lambda j, c: step(j, c, False), init)
    for x, (_, acc) in enumerate(step(qi, carry, True)):
        o_ref[x] = (acc[:head_dim] / acc[head_dim:head_dim + 1]).astype(o_ref.dtype)


def _attn_prompt(q_t, k_aug, v_aug, bounded):
    B, H, _, T = q_t.shape
    vr = v_aug.shape[2]
    head_dim = vr - V_EXTRA
    blk = ATT_BLOCK
    hb = ATT_HEADS
    return pl.pallas_call(
        functools.partial(_attn_prompt_kernel, blk=blk, head_dim=head_dim, bounded=bounded),
        grid=(B, H // hb, T // blk),
        in_specs=[
            pl.BlockSpec((None, hb, LANES, blk), lambda b, h, i: (b, h, 0, i)),
            pl.BlockSpec((None, T, hb * LANES), lambda b, h, i: (b, 0, h)),
            pl.BlockSpec((None, hb, vr, T), lambda b, h, i: (b, h, 0, 0)),
        ],
        out_specs=pl.BlockSpec((None, hb, head_dim, blk), lambda b, h, i: (b, h, 0, i)),
        out_shape=jax.ShapeDtypeStruct((B, H, head_dim, T), BF16),
        compiler_params=_params("arbitrary", "arbitrary", "arbitrary"),
        name="attn_prompt_bounded" if bounded else "attn_prompt",
    )(q_t, k_aug, v_aug)


def _attn_sample_kernel(pt_ref, q_ref, lfn_ref, kn_ref, vn_ref, *refs, n_pages, page, ts, head_dim, new_pad):
    del pt_ref
    lf_refs = refs[:n_pages]
    k_refs = refs[n_pages:2 * n_pages]
    v_refs = refs[2 * n_pages:3 * n_pages]
    o_ref = refs[3 * n_pages]
    D = q_ref.shape[1]
    n_pairs = D // LANES
    tp = 8
    lane = lax.broadcasted_iota(jnp.int32, (tp, LANES), 1)
    first = lane < head_dim
    tri = (lax.broadcasted_iota(jnp.int32, (page, page), 0)
           <= lax.broadcasted_iota(jnp.int32, (page, page), 1)).astype(BF16)

    q = q_ref[...].astype(F32)
    q8 = jnp.concatenate([q, jnp.zeros((tp - ts, D), F32)], axis=0)
    q_pairs = []
    for p in range(n_pairs):
        qp = q8[:, p * LANES:(p + 1) * LANES]
        q_pairs.append(jnp.concatenate([jnp.where(first, qp, 0.0), jnp.where(first, 0.0, qp)], axis=0).astype(BF16))

    def running_sum(lf_t, carry):
        parts = _dot(jnp.concatenate(_split3(lf_t), axis=0), tri)
        return parts[:N_HEADS] + parts[N_HEADS:2 * N_HEADS] + parts[2 * N_HEADS:] + carry

    def head_rows(c):
        n = c.shape[1]
        return jnp.broadcast_to(c[:, None, :], (N_HEADS, tp, n)).reshape(N_HEADS * tp, n)

    carry = jnp.zeros((N_HEADS, 1), F32)
    s_pages = []
    for j in range(n_pages):
        c = running_sum(lf_refs[j][...], carry)
        carry = c[:, page - 1:]
        s = [_dot(q_pairs[p], k_refs[j][p * LANES:(p + 1) * LANES, :].astype(BF16)) for p in range(n_pairs)]
        s_pages.append(jnp.concatenate(s, axis=0) - head_rows(c))
    c = running_sum(lfn_ref[...], carry)[:, :new_pad]
    s = [_dot_nt(q_pairs[p], kn_ref[:, p * LANES:(p + 1) * LANES]) for p in range(n_pairs)]
    t_row = lax.broadcasted_iota(jnp.int32, (N_HEADS * tp, new_pad), 0) % tp
    t_key = lax.broadcasted_iota(jnp.int32, (N_HEADS * tp, new_pad), 1)
    s_new = jnp.where(t_key <= t_row, jnp.concatenate(s, axis=0) - head_rows(c), NEG_BIG)

    m_el = s_pages[0]
    for s in s_pages[1:]:
        m_el = jnp.maximum(m_el, s)
    m = jnp.maximum(jnp.max(m_el, axis=1, keepdims=True), jnp.max(s_new, axis=1, keepdims=True))
    l_el = jnp.zeros((N_HEADS * tp, page), F32)
    acc_t = jnp.zeros((D, N_HEADS * tp), F32)
    for j in range(n_pages):
        pj = jnp.exp(s_pages[j] - m)
        l_el = l_el + pj
        acc_t = acc_t + _dot(v_refs[j][...].astype(BF16), pj.T.astype(BF16))
    p_new = jnp.exp(s_new - m)
    l = jnp.sum(l_el, axis=1, keepdims=True) + jnp.sum(p_new, axis=1, keepdims=True)
    o = (acc_t.T + _dot(p_new.astype(BF16), vn_ref[...])) / l
    row_head = lax.broadcasted_iota(jnp.int32, (N_HEADS * tp, D), 0) // tp
    col_head = lax.broadcasted_iota(jnp.int32, (N_HEADS * tp, D), 1) // head_dim
    own = jnp.where(row_head == col_head, o, 0.0).reshape(N_HEADS, tp, D)
    o_ref[...] = jnp.sum(own, axis=0)[:ts].astype(o_ref.dtype)


def _attn_sample(q, lf_new_t, k_new, v_new, cache_k, cache_v, cache_logf, page_table):
    bs, ts, D = q.shape
    n_pages = page_table.shape[1]
    n_pool, page = cache_k.shape[0], cache_k.shape[1]
    assert page == LANES
    new_pad = k_new.shape[1]
    ck = jnp.transpose(cache_k, (0, 2, 3, 1)).reshape(n_pool, D, page)
    cv = jnp.transpose(cache_v, (0, 2, 3, 1)).reshape(n_pool, D, page)
    cl = jnp.transpose(cache_logf, (0, 2, 1))
    seq = lambda n: pl.BlockSpec((None, n, D), lambda b, pt: (b, 0, 0))
    paged = lambda r: [pl.BlockSpec((None, r, page), lambda b, pt, j=j: (pt[b, j], 0, 0)) for j in range(n_pages)]
    return pl.pallas_call(
        functools.partial(_attn_sample_kernel, n_pages=n_pages, page=page, ts=ts,
                          head_dim=D // N_HEADS, new_pad=new_pad),
        grid_spec=pltpu.PrefetchScalarGridSpec(
            num_scalar_prefetch=1,
            grid=(bs,),
            in_specs=[seq(ts), pl.BlockSpec((None, N_HEADS, page), lambda b, pt: (b, 0, 0)),
                      seq(new_pad), seq(new_pad)] + paged(N_HEADS) + paged(D) + paged(D),
            out_specs=seq(ts),
        ),
        out_shape=jax.ShapeDtypeStruct((bs, ts, D), BF16),
        compiler_params=_params("arbitrary"),
        name="attn_sample",
    )(page_table, q, lf_new_t, k_new, v_new, *([cl] * n_pages), *([ck] * n_pages), *([cv] * n_pages))


def kernel(x_prompt, x_sample, state_pool, cache_k, cache_v, cache_logf, page_table, g_pool, w_pool, pool_scale,
           g_attn, w_q, g_q, w_o, g_kv, w_k, w_v, g_k, w_f, b_f, g_mlp, w_up, w_down):
    Bp, Tp, D = x_prompt.shape
    Bs, Ts, _ = x_sample.shape
    n_a = g_pool.shape[0]
    n_b = g_attn.shape[0]
    head_dim = D // N_HEADS
    past = page_table.shape[1] * cache_k.shape[1]
    new_pad = 16

    row = lambda a: a.reshape(1, -1).astype(F32)
    w_pool_b = w_pool.astype(BF16)
    w_up_b, w_down_b = w_up.astype(BF16), w_down.astype(BF16)
    w_q_b, w_o_b = w_q.astype(BF16), w_o.astype(BF16)
    w_k_b, w_v_b = w_k.astype(BF16), w_v.astype(BF16)
    w_f_p = jnp.pad(w_f, ((0, 0), (0, LANES - N_HEADS))).astype(BF16)
    b_f_p = jnp.pad(b_f.astype(F32), (0, LANES - N_HEADS)).reshape(1, LANES)
    e = (jnp.arange(D)[:, None] // head_dim == jnp.arange(LANES)[None, :]).astype(BF16)
    et = e.T
    g_k_t = row(jnp.tile(g_k, N_HEADS))

    h_p = x_prompt
    h_s = jnp.transpose(x_sample, (1, 0, 2))
    st_p, st_s = [], []
    for l in range(n_a):
        args = (row(g_pool[l]), w_pool_b[l], row(pool_scale[l]))
        h_p, tail = _pool_prompt(h_p, *args)
        st_p.append(tail[:, HALO - POOL_BUF:])
        h_p = _mlp(h_p.reshape(Bp * Tp, D), row(g_mlp[l]), w_up_b[l], w_down_b[l]).reshape(Bp, Tp, D)
        h_s, st = _pool_sample(h_s, jnp.transpose(state_pool[l], (1, 0, 2)), *args, past)
        st_s.append(jnp.transpose(st, (1, 0, 2)))
        h_s = _mlp(h_s.reshape(Ts * Bs, D), row(g_mlp[l]), w_up_b[l], w_down_b[l]).reshape(Ts, Bs, D)
    pool_state_prompt = jnp.stack(st_p, axis=0)
    pool_state_sample = jnp.stack(st_s, axis=0)

    h_p = h_p.reshape(Bp * Tp, D)
    h_s = jnp.transpose(h_s, (1, 0, 2)).reshape(Bs * Ts, D)

    kv_tail = (g_k_t, w_f_p, b_f_p, e, et)
    kt_p, vt_p, lft_p, ct_p, ka_p, va_p = _kv_prompt(h_p, Tp, row(g_kv), w_k_b, w_v_b.T, *kv_tail)
    k_s, v_s, lf_s, kb_s, vb_s = _kv_sample(h_s, row(g_kv), w_k_b, w_v_b, *kv_tail)
    pad_new = lambda a, n: jnp.pad(a.reshape(Bs, Ts, -1), ((0, 0), (0, n - Ts), (0, 0)))
    lfn_t = jnp.transpose(pad_new(lf_s, cache_logf.shape[1]), (0, 2, 1))
    kn, vn = pad_new(kb_s, new_pad), pad_new(vb_s, new_pad)
    ka_p = ka_p.reshape(Bp, Tp, N_HEADS * LANES)

    for l in range(n_b):
        j = n_a + l
        g_q_t = row(jnp.tile(g_q[l], N_HEADS))
        bound = (1.02 * LOG2E * head_dim ** 0.5) * jnp.max(jnp.abs(g_q[l])) * jnp.max(jnp.abs(g_k))
        q_p = _qproj(h_p, row(g_attn[l]), w_q_b[l], g_q_t, e, et, ct_p, bound.reshape(1, 1).astype(F32))
        o_p = lax.cond(2.0 * bound < MAX_SELF_DEFICIT,
                       functools.partial(_attn_prompt, bounded=True),
                       functools.partial(_attn_prompt, bounded=False), q_p, ka_p, va_p).reshape(Bp, D, Tp)
        h_p = _mlp(h_p, row(g_mlp[j]), w_up_b[j], w_down_b[j], o_p, w_o_b[l], o_transposed=True)
        q_s = _qproj(h_s, row(g_attn[l]), w_q_b[l], g_q_t, e, et)
        o_s = _attn_sample(q_s.reshape(Bs, Ts, D), lfn_t, kn, vn, cache_k, cache_v, cache_logf, page_table)
        h_s = _mlp(h_s, row(g_mlp[j]), w_up_b[j], w_down_b[j], o_s.reshape(Bs * Ts, D), w_o_b[l])

    hd_t = lambda a: jnp.transpose(a.reshape(Bp, N_HEADS, head_dim, Tp), (0, 3, 1, 2))
    hd4 = lambda a: a.reshape(Bs, Ts, N_HEADS, head_dim)
    return (h_p.reshape(Bp, Tp, D), h_s.reshape(Bs, Ts, D), pool_state_prompt, pool_state_sample,
            hd_t(kt_p), hd_t(vt_p), jnp.transpose(lft_p, (0, 2, 1)).astype(cache_logf.dtype),
            hd4(k_s), hd4(v_s), lf_s.reshape(Bs, Ts, N_HEADS).astype(cache_logf.dtype))
```

```python
import functools

import jax
import jax.numpy as jnp
from jax import lax
from jax.experimental import pallas as pl
from jax.experimental.pallas import tpu as pltpu

F32 = jnp.float32
BF16 = jnp.bfloat16

EPS = 1e-6
POOL_WINDOWS = (2, 4, 8, 16)
POOL_BUF = max(POOL_WINDOWS) - 1
N_HEADS = 16
LANES = 128
HALO = 16
VMEM_LIMIT = 56 * 1024 * 1024
NEG_BIG = -1e30

TOK_BLOCK = 512
FF_CHUNK = 1024
PROJ_SPLITS = 4
ATT_BLOCK = 512
ATT_UNROLL = 4
ATT_HEADS = 4
V_EXTRA = 16
LOG2E = 1.4426950408889634
MAX_SELF_DEFICIT = 120.0


def _dot(a, b):
    return jnp.dot(a, b, preferred_element_type=F32)


def _dot_nt(a, b):
    return lax.dot_general(a, b, (((1,), (1,)), ((), ())), preferred_element_type=F32)


def _rms(x, g):
    ms = jnp.mean(x * x, axis=-1, keepdims=True)
    return x * lax.rsqrt(ms + EPS) * g


def _split2(x):
    hi = x.astype(BF16)
    lo = (x - hi.astype(F32)).astype(BF16)
    return hi, lo


def _split3(x):
    hi = x.astype(BF16)
    r1 = x - hi.astype(F32)
    mid = r1.astype(BF16)
    lo = (r1 - mid.astype(F32)).astype(BF16)
    return hi, mid, lo


def _head_rms(x, g_tiled, e, et, head_dim):
    hi, lo = _split2(x * x)
    ss = _dot(hi, e) + _dot(lo, e)
    r = lax.rsqrt(ss * (1.0 / head_dim) + EPS)
    rh, rl = _split2(r)
    return x * (_dot(rh, et) + _dot(rl, et)) * g_tiled


def _proj_head_rms(x, w_ref, g_ref, e_ref, et_ref, head_dim):
    width = w_ref.shape[1] // PROJ_SPLITS
    cols = [slice(i * width, (i + 1) * width) for i in range(PROJ_SPLITS)]
    ys = [_dot(x, w_ref[:, c]) for c in cols]
    return jnp.concatenate([_head_rms(y, g_ref[:, c], e_ref[c, :], et_ref[:, c], head_dim)
                            for y, c in zip(ys, cols)], axis=1)


def _const_spec(shape):
    return pl.BlockSpec(shape, lambda *_: (0,) * len(shape), pipeline_mode=pl.Buffered(1))


def _params(*sem):
    return pltpu.CompilerParams(dimension_semantics=sem, vmem_limit_bytes=VMEM_LIMIT)


def _pool_prompt_kernel(x_ref, halo_ref, g_ref, w_ref, sc_ref, o_ref, st_ref, *, tb, gdim):
    i = pl.program_id(1)
    g = g_ref[...]
    x = x_ref[...]
    xn = _rms(x, g)
    hn = _rms(halo_ref[...], g)
    sums = jnp.concatenate([jnp.where(i > 0, hn, 0.0), xn], axis=0)
    width, windows = 1, []
    for gi, w in enumerate(POOL_WINDOWS):
        while width < w:
            sums = sums + pltpu.roll(sums, width, 0)
            width *= 2
        assert width == w
        windows.append(sums[HALO:, :gdim])
        sums = sums[:, gdim:]
    pos = i * tb + lax.broadcasted_iota(jnp.int32, (tb, 1), 0)
    sc = sc_ref[...]
    for gi, w in enumerate(POOL_WINDOWS):
        sl = slice(gi * gdim, (gi + 1) * gdim)
        cnt = jnp.minimum(pos + 1, w).astype(F32)
        pooled = windows[gi] / cnt - xn[:, sl]
        mixed = _dot(pooled.astype(BF16), w_ref[gi]) * sc[:, sl]
        o_ref[:, sl] = x[:, sl] + mixed
    st_ref[...] = xn[tb - HALO:, :]


def _pool_prompt(h, g, w, sc):
    B, T, D = h.shape
    tb = TOK_BLOCK
    gdim = D // len(POOL_WINDOWS)
    r = tb // HALO
    return pl.pallas_call(
        functools.partial(_pool_prompt_kernel, tb=tb, gdim=gdim),
        grid=(B, T // tb),
        in_specs=[
            pl.BlockSpec((None, tb, D), lambda b, i: (b, i, 0)),
            pl.BlockSpec((None, HALO, D), lambda b, i: (b, jnp.maximum(i * r - 1, 0), 0)),
            _const_spec((1, D)),
            _const_spec((len(POOL_WINDOWS), gdim, gdim)),
            _const_spec((1, D)),
        ],
        out_specs=[
            pl.BlockSpec((None, tb, D), lambda b, i: (b, i, 0)),
            pl.BlockSpec((None, HALO, D), lambda b, i: (b, 0, 0)),
        ],
        out_shape=[jax.ShapeDtypeStruct((B, T, D), F32), jax.ShapeDtypeStruct((B, HALO, D), F32)],
        compiler_params=_params("arbitrary", "arbitrary"),
        name="pool_prompt",
    )(h, h, g, w, sc)


def _pool_sample_kernel(x_ref, pf_ref, g_ref, w_ref, sc_ref, o_ref, st_ref, *, ts, bb, gdim, cnts):
    g = g_ref[...]
    sc = sc_ref[...]
    xs = [x_ref[t] for t in range(ts)]
    xn = [_rms(x, g) for x in xs]
    ext = [pf_ref[r] for r in range(POOL_BUF)] + xn
    for gi, w in enumerate(POOL_WINDOWS):
        sl = slice(gi * gdim, (gi + 1) * gdim)
        pooled = []
        for t in range(ts):
            acc = ext[POOL_BUF + t][:, sl]
            for j in range(1, w):
                acc = acc + ext[POOL_BUF + t - j][:, sl]
            pooled.append(acc / cnts[t][gi] - xn[t][:, sl])
        mixed = _dot(jnp.concatenate(pooled, axis=0).astype(BF16), w_ref[gi]) * sc[:, sl]
        for t in range(ts):
            o_ref[t, :, sl] = xs[t][:, sl] + mixed[t * bb:(t + 1) * bb]
    for r in range(POOL_BUF):
        st_ref[r] = ext[ts + r]


def _pool_sample(h_tm, prefix_tm, g, w, sc, past):
    ts, bs, D = h_tm.shape
    bb = 32
    gdim = D // len(POOL_WINDOWS)
    cnts = tuple(tuple(float(min(past + t + 1, w)) for w in POOL_WINDOWS) for t in range(ts))
    return pl.pallas_call(
        functools.partial(_pool_sample_kernel, ts=ts, bb=bb, gdim=gdim, cnts=cnts),
        grid=(bs // bb,),
        in_specs=[
            pl.BlockSpec((ts, bb, D), lambda i: (0, i, 0)),
            pl.BlockSpec((POOL_BUF, bb, D), lambda i: (0, i, 0)),
            _const_spec((1, D)),
            _const_spec((len(POOL_WINDOWS), gdim, gdim)),
            _const_spec((1, D)),
        ],
        out_specs=[
            pl.BlockSpec((ts, bb, D), lambda i: (0, i, 0)),
            pl.BlockSpec((POOL_BUF, bb, D), lambda i: (0, i, 0)),
        ],
        out_shape=[jax.ShapeDtypeStruct((ts, bs, D), F32), jax.ShapeDtypeStruct((POOL_BUF, bs, D), F32)],
        compiler_params=_params("arbitrary"),
        name="pool_sample",
    )(h_tm, prefix_tm, g, w, sc)


def _mlp_kernel(*refs, o_mode, n_chunks):
    if o_mode is None:
        h_ref, g_ref, wu_ref, wd_ref, out_ref = refs
        h = h_ref[...]
    else:
        h_ref, o_ref, wo_ref, g_ref, wu_ref, wd_ref, out_ref = refs
        o = o_ref[...]
        if o_mode == "transposed":
            proj = lax.dot_general(o, wo_ref[...], (((0,), (0,)), ((), ())), preferred_element_type=F32)
        else:
            proj = _dot(o, wo_ref[...])
        h = h_ref[...] + proj
    xn = _rms(h, g_ref[...]).astype(BF16)
    acc = h
    for c in range(n_chunks):
        cs = slice(c * FF_CHUNK, (c + 1) * FF_CHUNK)
        a = jnp.maximum(_dot(xn, wu_ref[:, cs]), 0.0)
        acc = acc + _dot((a * a).astype(BF16), wd_ref[cs, :])
    out_ref[...] = acc


def _mlp(h, g, w_up, w_down, o=None, w_o=None, o_transposed=False):
    N, D = h.shape
    F = w_up.shape[1]
    tm = TOK_BLOCK
    tok = pl.BlockSpec((tm, D), lambda i: (i, 0))
    in_specs = [tok]
    args = [h]
    o_mode = None
    if o is not None:
        if o_transposed:
            o_mode = "transposed"
            nb = o.shape[2] // tm
            in_specs.append(pl.BlockSpec((None, D, tm), lambda i: (i // nb, 0, i % nb)))
        else:
            o_mode = "rows"
            in_specs.append(tok)
        in_specs.append(_const_spec((D, D)))
        args += [o, w_o]
    in_specs += [_const_spec((1, D)), _const_spec((D, F)), _const_spec((F, D))]
    args += [g, w_up, w_down]
    return pl.pallas_call(
        functools.partial(_mlp_kernel, o_mode=o_mode, n_chunks=F // FF_CHUNK),
        grid=(N // tm,),
        in_specs=in_specs,
        out_specs=tok,
        out_shape=jax.ShapeDtypeStruct((N, D), F32),
        compiler_params=_params("arbitrary"),
        name="mlp" if o is None else "mlp_o",
    )(*args)


def _log_sigmoid(z):
    return -(jnp.maximum(-z, 0.0) + jnp.log1p(jnp.exp(-jnp.abs(z))))


def _kv_core(h_ref, g_ref, wk_ref, wv_ref, gk_ref, wf_ref, bf_ref, e_ref, et_ref, head_dim, v_transposed):
    hn = _rms(h_ref[...], g_ref[...]).astype(BF16)
    k = _proj_head_rms(hn, wk_ref, gk_ref, e_ref, et_ref, head_dim)
    v = _dot_nt(wv_ref[...], hn) if v_transposed else _dot(hn, wv_ref[...])
    lf = _log_sigmoid(_dot(hn, wf_ref[...]) + bf_ref[...])
    return k, v, lf


def _kv_sample_kernel(h_ref, g_ref, wk_ref, wv_ref, gk_ref, wf_ref, bf_ref, e_ref, et_ref,
                      k_ref, v_ref, lf_ref, kb_ref, vb_ref, *, head_dim):
    k, v, lf = _kv_core(h_ref, g_ref, wk_ref, wv_ref, gk_ref, wf_ref, bf_ref, e_ref, et_ref, head_dim, False)
    k_ref[...] = k
    v_ref[...] = v
    kb_ref[...] = k.astype(BF16)
    vb_ref[...] = v.astype(BF16)
    lf_ref[...] = lf[:, :N_HEADS]


def _kv_sample(h, g, wk, wv, gk_t, wf_p, bf_p, e, et):
    N, D = h.shape
    tm = TOK_BLOCK
    tok = pl.BlockSpec((tm, D), lambda i: (i, 0))
    return pl.pallas_call(
        functools.partial(_kv_sample_kernel, head_dim=D // N_HEADS),
        grid=(N // tm,),
        in_specs=[tok, _const_spec((1, D)), _const_spec((D, D)), _const_spec((D, D)), _const_spec((1, D)),
                  _const_spec((D, LANES)), _const_spec((1, LANES)), _const_spec((D, LANES)),
                  _const_spec((LANES, D))],
        out_specs=[tok, tok, pl.BlockSpec((tm, N_HEADS), lambda i: (i, 0)), tok, tok],
        out_shape=[jax.ShapeDtypeStruct((N, D), F32), jax.ShapeDtypeStruct((N, D), F32),
                   jax.ShapeDtypeStruct((N, N_HEADS), F32),
                   jax.ShapeDtypeStruct((N, D), BF16), jax.ShapeDtypeStruct((N, D), BF16)],
        compiler_params=_params("arbitrary"),
        name="shared_kv_sample",
    )(h, g, wk, wv, gk_t, wf_p, bf_p, e, et)


def _kv_prompt_kernel(h_ref, g_ref, wk_ref, wv_ref, gk_ref, wf_ref, bf_ref, e_ref, et_ref, tril_ref, spread_ref,
                      kt_ref, vt_ref, lft_ref, ct_ref, ka_ref, va_ref, carry_ref, *, nb, head_dim):
    i = pl.program_id(0)
    k, v_t, lf = _kv_core(h_ref, g_ref, wk_ref, wv_ref, gk_ref, wf_ref, bf_ref, e_ref, et_ref, head_dim, True)
    tm = k.shape[0]
    lane = lax.broadcasted_iota(jnp.int32, (tm, LANES), 1)
    lf = jnp.where(lane < N_HEADS, lf, 0.0)
    kt_ref[...] = k.T
    vt_ref[...] = v_t
    lft_ref[...] = lf.T[:N_HEADS]

    @pl.when(i % nb == 0)
    def _():
        carry_ref[...] = jnp.zeros_like(carry_ref)

    c = carry_ref[...]
    for part in _split3(lf):
        c = c + _dot(tril_ref[...], part)
    carry_ref[...] = c[tm - 1:, :]
    ct_ref[...] = c.T[:N_HEADS]

    hi, mid, lo = (t.astype(F32) for t in _split3(c * -LOG2E))
    terms = hi + pltpu.roll(mid, N_HEADS, 1) + pltpu.roll(lo, 2 * N_HEADS, 1)
    ones = jnp.where((lane >= head_dim + 3) & (lane < head_dim + 6), 1.0, 0.0)
    aug = _dot(terms.astype(BF16), spread_ref[...])
    low = lane < head_dim
    for p in range(k.shape[1] // LANES):
        kp = k[:, p * LANES:(p + 1) * LANES]
        even = jnp.where(low, kp, ones) + aug[:, (2 * p) * LANES:(2 * p + 1) * LANES]
        odd = jnp.where(low, pltpu.roll(kp, head_dim, 1), ones) + aug[:, (2 * p + 1) * LANES:(2 * p + 2) * LANES]
        ka_ref[:, (2 * p) * LANES:(2 * p + 1) * LANES] = even.astype(BF16)
        ka_ref[:, (2 * p + 1) * LANES:(2 * p + 2) * LANES] = odd.astype(BF16)

    va_ref[:, 0:head_dim, :] = v_t.reshape(N_HEADS, head_dim, tm).astype(BF16)
    sub = lax.broadcasted_iota(jnp.int32, (N_HEADS, V_EXTRA, tm), 1)
    va_ref[:, head_dim:, :] = jnp.where(sub == 0, 1.0, 0.0).astype(BF16)


def _kv_prompt(h, T, g, wk, wv_t, gk_t, wf_p, bf_p, e, et):
    N, D = h.shape
    B = N // T
    head_dim = D // N_HEADS
    tm = TOK_BLOCK
    nb = T // tm
    tok = pl.BlockSpec((tm, D), lambda i: (i, 0))
    tril = (jnp.arange(tm)[:, None] >= jnp.arange(tm)[None, :]).astype(BF16)
    src = jnp.arange(LANES)[:, None]
    dst = jnp.arange(N_HEADS * LANES)[None, :]
    spread = ((src < 3 * N_HEADS) & (dst // LANES == src % N_HEADS)
              & (dst % LANES == head_dim + src // N_HEADS)).astype(BF16)
    seq_t = lambda r: pl.BlockSpec((None, r, tm), lambda i: (i // nb, 0, i % nb))
    return pl.pallas_call(
        functools.partial(_kv_prompt_kernel, nb=nb, head_dim=head_dim),
        grid=(N // tm,),
        in_specs=[tok, _const_spec((1, D)), _const_spec((D, D)), _const_spec((D, D)), _const_spec((1, D)),
                  _const_spec((D, LANES)), _const_spec((1, LANES)), _const_spec((D, LANES)),
                  _const_spec((LANES, D)), _const_spec((tm, tm)), _const_spec((LANES, N_HEADS * LANES))],
        out_specs=[seq_t(D), seq_t(D), seq_t(N_HEADS), seq_t(N_HEADS),
                   pl.BlockSpec((tm, N_HEADS * LANES), lambda i: (i, 0)),
                   pl.BlockSpec((None, N_HEADS, head_dim + V_EXTRA, tm), lambda i: (i // nb, 0, 0, i % nb))],
        out_shape=[jax.ShapeDtypeStruct((B, D, T), F32), jax.ShapeDtypeStruct((B, D, T), F32),
                   jax.ShapeDtypeStruct((B, N_HEADS, T), F32), jax.ShapeDtypeStruct((B, N_HEADS, T), F32),
                   jax.ShapeDtypeStruct((N, N_HEADS * LANES), BF16),
                   jax.ShapeDtypeStruct((B, N_HEADS, head_dim + V_EXTRA, T), BF16)],
        scratch_shapes=[pltpu.VMEM((1, LANES), F32)],
        compiler_params=_params("arbitrary"),
        name="shared_kv_prompt",
    )(h, g, wk, wv_t, gk_t, wf_p, bf_p, e, et, tril, spread)


def _q_kernel(*refs, head_dim, transposed):
    if transposed:
        h_ref, g_ref, wq_ref, gq_ref, e_ref, et_ref, ct_ref, bound_ref, q_ref = refs
    else:
        h_ref, g_ref, wq_ref, gq_ref, e_ref, et_ref, q_ref = refs
    xn = _rms(h_ref[...], g_ref[...]).astype(BF16)
    q = _proj_head_rms(xn, wq_ref, gq_ref, e_ref, et_ref, head_dim)
    if not transposed:
        q_ref[...] = (q * (head_dim ** -0.5)).astype(BF16)
        return
    tm = q.shape[0]
    q_t = (q * (head_dim ** -0.5 * LOG2E)).T.reshape(N_HEADS, head_dim, tm)
    q_ref[:, 0:head_dim, :] = q_t.astype(BF16)
    shape = (N_HEADS, LANES - head_dim, tm)
    sub = lax.broadcasted_iota(jnp.int32, shape, 1)
    hi, mid, lo = (jnp.broadcast_to(t.astype(F32)[:, None, :], shape)
                   for t in _split3(ct_ref[...] * LOG2E - bound_ref[...]))
    tail = jnp.where(sub < 3, 1.0, jnp.where(sub == 3, hi, jnp.where(sub == 4, mid, jnp.where(sub == 5, lo, 0.0))))
    q_ref[:, head_dim:, :] = tail.astype(BF16)


def _qproj(h, g, wq, gq_t, e, et, ct=None, bound=None):
    N, D = h.shape
    tm = TOK_BLOCK
    tok = pl.BlockSpec((tm, D), lambda i: (i, 0))
    in_specs = [tok, _const_spec((1, D)), _const_spec((D, D)), _const_spec((1, D)),
                _const_spec((D, LANES)), _const_spec((LANES, D))]
    args = [h, g, wq, gq_t, e, et]
    if ct is None:
        out_spec, out_shape = tok, jax.ShapeDtypeStruct((N, D), BF16)
    else:
        T = ct.shape[2]
        nb = T // tm
        in_specs += [pl.BlockSpec((None, N_HEADS, tm), lambda i: (i // nb, 0, i % nb)), _const_spec((1, 1))]
        args += [ct, bound]
        out_spec = pl.BlockSpec((None, N_HEADS, LANES, tm), lambda i: (i // nb, 0, 0, i % nb))
        out_shape = jax.ShapeDtypeStruct((N // T, N_HEADS, LANES, T), BF16)
    return pl.pallas_call(
        functools.partial(_q_kernel, head_dim=D // N_HEADS, transposed=ct is not None),
        grid=(N // tm,),
        in_specs=in_specs,
        out_specs=out_spec,
        out_shape=out_shape,
        compiler_params=_params("arbitrary"),
        name="q_proj" if ct is None else "q_proj_t",
    )(*args)


def _attn_prompt_kernel(q_ref, k_ref, v_ref, o_ref, *, blk, head_dim, bounded):
    qi = pl.program_id(2)
    n_h = q_ref.shape[0]
    q_t = [q_ref[x] for x in range(n_h)]
    key_iota = lax.broadcasted_iota(jnp.int32, (blk, blk), 0)
    qry_iota = lax.broadcasted_iota(jnp.int32, (blk, blk), 1)

    def step(j0, n_blocks, carry, diagonal):
        offs = [pl.multiple_of((j0 + b) * blk, blk) for b in range(n_blocks)]
        s_all = [[_dot(k_ref[pl.ds(off, blk), x * LANES:(x + 1) * LANES], q_t[x]) for x in range(n_h)]
                 for off in offs]
        out = []
        for x, (m, acc) in enumerate(carry):
            for b, off in enumerate(offs):
                s = s_all[b][x]
                if diagonal:
                    s = jnp.where(key_iota <= qry_iota, s, NEG_BIG)
                if bounded:
                    acc = acc + _dot(v_ref[x, :, pl.ds(off, blk)], jnp.exp2(s).astype(BF16))
                    continue
                m_new = jnp.maximum(m, jnp.max(s, axis=0, keepdims=True))
                p = jnp.exp2(s - m_new).astype(BF16)
                acc = jnp.exp2(m - m_new) * acc + _dot(v_ref[x, :, pl.ds(off, blk)], p)
                m = m_new
            out.append((m, acc))
        return tuple(out)

    init = tuple((jnp.full((1, blk), NEG_BIG, F32), jnp.zeros((v_ref.shape[1], blk), F32)) for _ in range(n_h))
    n_wide = qi // ATT_UNROLL
    carry = lax.fori_loop(0, n_wide, lambda jj, c: step(jj * ATT_UNROLL, ATT_UNROLL, c, False), init)
    carry = lax.fori_loop(n_wide * ATT_UNROLL, qi, lambda j, c: step(j, 1, c, False), carry)
    for x, (_, acc) in enumerate(step(qi, 1, carry, True)):
        o_ref[x] = (acc[:head_dim] / acc[head_dim:head_dim + 1]).astype(o_ref.dtype)


def _attn_prompt(q_t, k_aug, v_aug, bounded):
    B, H, _, T = q_t.shape
    vr = v_aug.shape[2]
    head_dim = vr - V_EXTRA
    blk = ATT_BLOCK
    hb = ATT_HEADS
    return pl.pallas_call(
        functools.partial(_attn_prompt_kernel, blk=blk, head_dim=head_dim, bounded=bounded),
        grid=(B, H // hb, T // blk),
        in_specs=[
            pl.BlockSpec((None, hb, LANES, blk), lambda b, h, i: (b, h, 0, i)),
            pl.BlockSpec((None, T, hb * LANES), lambda b, h, i: (b, 0, h)),
            pl.BlockSpec((None, hb, vr, T), lambda b, h, i: (b, h, 0, 0)),
        ],
        out_specs=pl.BlockSpec((None, hb, head_dim, blk), lambda b, h, i: (b, h, 0, i)),
        out_shape=jax.ShapeDtypeStruct((B, H, head_dim, T), BF16),
        compiler_params=_params("arbitrary", "arbitrary", "arbitrary"),
        name="attn_prompt_bounded" if bounded else "attn_prompt",
    )(q_t, k_aug, v_aug)


def _attn_sample_kernel(pt_ref, q_ref, lfn_ref, kn_ref, vn_ref, *refs, n_pages, page, ts, head_dim, new_pad):
    del pt_ref
    lf_refs = refs[:n_pages]
    k_refs = refs[n_pages:2 * n_pages]
    v_refs = refs[2 * n_pages:3 * n_pages]
    o_ref = refs[3 * n_pages]
    D = q_ref.shape[1]
    n_pairs = D // LANES
    tp = 8
    lane = lax.broadcasted_iota(jnp.int32, (tp, LANES), 1)
    first = lane < head_dim
    tri = (lax.broadcasted_iota(jnp.int32, (page, page), 0)
           <= lax.broadcasted_iota(jnp.int32, (page, page), 1)).astype(BF16)

    q = q_ref[...].astype(F32)
    q8 = jnp.concatenate([q, jnp.zeros((tp - ts, D), F32)], axis=0)
    q_pairs = []
    for p in range(n_pairs):
        qp = q8[:, p * LANES:(p + 1) * LANES]
        q_pairs.append(jnp.concatenate([jnp.where(first, qp, 0.0), jnp.where(first, 0.0, qp)], axis=0).astype(BF16))

    def running_sum(lf_t, carry):
        parts = _dot(jnp.concatenate(_split3(lf_t), axis=0), tri)
        return parts[:N_HEADS] + parts[N_HEADS:2 * N_HEADS] + parts[2 * N_HEADS:] + carry

    def head_rows(c):
        n = c.shape[1]
        return jnp.broadcast_to(c[:, None, :], (N_HEADS, tp, n)).reshape(N_HEADS * tp, n)

    carry = jnp.zeros((N_HEADS, 1), F32)
    s_pages = []
    for j in range(n_pages):
        c = running_sum(lf_refs[j][...], carry)
        carry = c[:, page - 1:]
        s = [_dot(q_pairs[p], k_refs[j][p * LANES:(p + 1) * LANES, :].astype(BF16)) for p in range(n_pairs)]
        s_pages.append(jnp.concatenate(s, axis=0) - head_rows(c))
    c = running_sum(lfn_ref[...], carry)[:, :new_pad]
    s = [_dot_nt(q_pairs[p], kn_ref[:, p * LANES:(p + 1) * LANES]) for p in range(n_pairs)]
    t_row = lax.broadcasted_iota(jnp.int32, (N_HEADS * tp, new_pad), 0) % tp
    t_key = lax.broadcasted_iota(jnp.int32, (N_HEADS * tp, new_pad), 1)
    s_new = jnp.where(t_key <= t_row, jnp.concatenate(s, axis=0) - head_rows(c), NEG_BIG)

    m_el = s_pages[0]
    for s in s_pages[1:]:
        m_el = jnp.maximum(m_el, s)
    m = jnp.maximum(jnp.max(m_el, axis=1, keepdims=True), jnp.max(s_new, axis=1, keepdims=True))
    l_el = jnp.zeros((N_HEADS * tp, page), F32)
    acc_t = jnp.zeros((D, N_HEADS * tp), F32)
    for j in range(n_pages):
        pj = jnp.exp(s_pages[j] - m)
        l_el = l_el + pj
        acc_t = acc_t + _dot(v_refs[j][...].astype(BF16), pj.T.astype(BF16))
    p_new = jnp.exp(s_new - m)
    l = jnp.sum(l_el, axis=1, keepdims=True) + jnp.sum(p_new, axis=1, keepdims=True)
    o = (acc_t.T + _dot(p_new.astype(BF16), vn_ref[...])) / l
    row_head = lax.broadcasted_iota(jnp.int32, (N_HEADS * tp, D), 0) // tp
    col_head = lax.broadcasted_iota(jnp.int32, (N_HEADS * tp, D), 1) // head_dim
    own = jnp.where(row_head == col_head, o, 0.0).reshape(N_HEADS, tp, D)
    o_ref[...] = jnp.sum(own, axis=0)[:ts].astype(o_ref.dtype)


def _attn_sample(q, lf_new_t, k_new, v_new, cache_k, cache_v, cache_logf, page_table):
    bs, ts, D = q.shape
    n_pages = page_table.shape[1]
    n_pool, page = cache_k.shape[0], cache_k.shape[1]
    assert page == LANES
    new_pad = k_new.shape[1]
    ck = jnp.transpose(cache_k, (0, 2, 3, 1)).reshape(n_pool, D, page)
    cv = jnp.transpose(cache_v, (0, 2, 3, 1)).reshape(n_pool, D, page)
    cl = jnp.transpose(cache_logf, (0, 2, 1))
    seq = lambda n: pl.BlockSpec((None, n, D), lambda b, pt: (b, 0, 0))
    paged = lambda r: [pl.BlockSpec((None, r, page), lambda b, pt, j=j: (pt[b, j], 0, 0)) for j in range(n_pages)]
    return pl.pallas_call(
        functools.partial(_attn_sample_kernel, n_pages=n_pages, page=page, ts=ts,
                          head_dim=D // N_HEADS, new_pad=new_pad),
        grid_spec=pltpu.PrefetchScalarGridSpec(
            num_scalar_prefetch=1,
            grid=(bs,),
            in_specs=[seq(ts), pl.BlockSpec((None, N_HEADS, page), lambda b, pt: (b, 0, 0)),
                      seq(new_pad), seq(new_pad)] + paged(N_HEADS) + paged(D) + paged(D),
            out_specs=seq(ts),
        ),
        out_shape=jax.ShapeDtypeStruct((bs, ts, D), BF16),
        compiler_params=_params("arbitrary"),
        name="attn_sample",
    )(page_table, q, lf_new_t, k_new, v_new, *([cl] * n_pages), *([ck] * n_pages), *([cv] * n_pages))


def kernel(x_prompt, x_sample, state_pool, cache_k, cache_v, cache_logf, page_table, g_pool, w_pool, pool_scale,
           g_attn, w_q, g_q, w_o, g_kv, w_k, w_v, g_k, w_f, b_f, g_mlp, w_up, w_down):
    Bp, Tp, D = x_prompt.shape
    Bs, Ts, _ = x_sample.shape
    n_a = g_pool.shape[0]
    n_b = g_attn.shape[0]
    head_dim = D // N_HEADS
    past = page_table.shape[1] * cache_k.shape[1]
    new_pad = 16

    row = lambda a: a.reshape(1, -1).astype(F32)
    layers = lambda w: [w[i].astype(BF16) for i in range(w.shape[0])]
    w_pool_b, w_up_b, w_down_b, w_q_b, w_o_b = map(layers, (w_pool, w_up, w_down, w_q, w_o))
    w_k_b, w_v_b = w_k.astype(BF16), w_v.astype(BF16)
    w_f_p = jnp.pad(w_f, ((0, 0), (0, LANES - N_HEADS))).astype(BF16)
    b_f_p = jnp.pad(b_f.astype(F32), (0, LANES - N_HEADS)).reshape(1, LANES)
    e = (jnp.arange(D)[:, None] // head_dim == jnp.arange(LANES)[None, :]).astype(BF16)
    et = e.T
    g_k_t = row(jnp.tile(g_k, N_HEADS))

    h_p = x_prompt
    h_s = jnp.transpose(x_sample, (1, 0, 2))
    st_p, st_s = [], []
    for l in range(n_a):
        args = (row(g_pool[l]), w_pool_b[l], row(pool_scale[l]))
        h_p, tail = _pool_prompt(h_p, *args)
        st_p.append(tail[:, HALO - POOL_BUF:])
        h_p = _mlp(h_p.reshape(Bp * Tp, D), row(g_mlp[l]), w_up_b[l], w_down_b[l]).reshape(Bp, Tp, D)
        h_s, st = _pool_sample(h_s, jnp.transpose(state_pool[l], (1, 0, 2)), *args, past)
        st_s.append(jnp.transpose(st, (1, 0, 2)))
        h_s = _mlp(h_s.reshape(Ts * Bs, D), row(g_mlp[l]), w_up_b[l], w_down_b[l]).reshape(Ts, Bs, D)
    pool_state_prompt = jnp.stack(st_p, axis=0)
    pool_state_sample = jnp.stack(st_s, axis=0)

    h_p = h_p.reshape(Bp * Tp, D)
    h_s = jnp.transpose(h_s, (1, 0, 2)).reshape(Bs * Ts, D)

    kv_tail = (g_k_t, w_f_p, b_f_p, e, et)
    kt_p, vt_p, lft_p, ct_p, ka_p, va_p = _kv_prompt(h_p, Tp, row(g_kv), w_k_b, w_v_b.T, *kv_tail)
    k_s, v_s, lf_s, kb_s, vb_s = _kv_sample(h_s, row(g_kv), w_k_b, w_v_b, *kv_tail)
    pad_new = lambda a, n: jnp.pad(a.reshape(Bs, Ts, -1), ((0, 0), (0, n - Ts), (0, 0)))
    lfn_t = jnp.transpose(pad_new(lf_s, cache_logf.shape[1]), (0, 2, 1))
    kn, vn = pad_new(kb_s, new_pad), pad_new(vb_s, new_pad)
    ka_p = ka_p.reshape(Bp, Tp, N_HEADS * LANES)

    for l in range(n_b):
        j = n_a + l
        g_q_t = row(jnp.tile(g_q[l], N_HEADS))
        bound = (1.02 * LOG2E * head_dim ** 0.5) * jnp.max(jnp.abs(g_q[l])) * jnp.max(jnp.abs(g_k))
        q_p = _qproj(h_p, row(g_attn[l]), w_q_b[l], g_q_t, e, et, ct_p, bound.reshape(1, 1).astype(F32))
        o_p = lax.cond(2.0 * bound < MAX_SELF_DEFICIT,
                       functools.partial(_attn_prompt, bounded=True),
                       functools.partial(_attn_prompt, bounded=False), q_p, ka_p, va_p).reshape(Bp, D, Tp)
        h_p = _mlp(h_p, row(g_mlp[j]), w_up_b[j], w_down_b[j], o_p, w_o_b[l], o_transposed=True)
        q_s = _qproj(h_s, row(g_attn[l]), w_q_b[l], g_q_t, e, et)
        o_s = _attn_sample(q_s.reshape(Bs, Ts, D), lfn_t, kn, vn, cache_k, cache_v, cache_logf, page_table)
        h_s = _mlp(h_s, row(g_mlp[j]), w_up_b[j], w_down_b[j], o_s.reshape(Bs * Ts, D), w_o_b[l])

    hd_t = lambda a: jnp.transpose(a.reshape(Bp, N_HEADS, head_dim, Tp), (0, 3, 1, 2))
    hd4 = lambda a: a.reshape(Bs, Ts, N_HEADS, head_dim)
    return (h_p.reshape(Bp, Tp, D), h_s.reshape(Bs, Ts, D), pool_state_prompt, pool_state_sample,
            hd_t(kt_p), hd_t(vt_p), jnp.transpose(lft_p, (0, 2, 1)).astype(cache_logf.dtype),
            hd4(k_s), hd4(v_s), lf_s.reshape(Bs, Ts, N_HEADS).astype(cache_logf.dtype))
```

```python
import functools

import jax
import jax.numpy as jnp
from jax import lax
from jax.experimental import pallas as pl
from jax.experimental.pallas import tpu as pltpu

F32 = jnp.float32
BF16 = jnp.bfloat16

EPS = 1e-6
POOL_WINDOWS = (2, 4, 8, 16)
POOL_BUF = max(POOL_WINDOWS) - 1
N_HEADS = 16
LANES = 128
HALO = 16
VMEM_LIMIT = 56 * 1024 * 1024
NEG_BIG = -1e30

TOK_BLOCK = 512
FF_CHUNK = 1024
PROJ_SPLITS = 4
ATT_BLOCK = 512
ATT_UNROLL = 4
ATT_HEADS = 4
V_EXTRA = 16
LOG2E = 1.4426950408889634
MAX_SELF_DEFICIT = 120.0


def _dot(a, b):
    return jnp.dot(a, b, preferred_element_type=F32)


def _dot_nt(a, b):
    return lax.dot_general(a, b, (((1,), (1,)), ((), ())), preferred_element_type=F32)


def _rms(x, g):
    ms = jnp.mean(x * x, axis=-1, keepdims=True)
    return x * lax.rsqrt(ms + EPS) * g


def _split2(x):
    hi = x.astype(BF16)
    lo = (x - hi.astype(F32)).astype(BF16)
    return hi, lo


def _split3(x):
    hi = x.astype(BF16)
    r1 = x - hi.astype(F32)
    mid = r1.astype(BF16)
    lo = (r1 - mid.astype(F32)).astype(BF16)
    return hi, mid, lo


def _head_rms(x, g_tiled, e, et, head_dim):
    hi, lo = _split2(x * x)
    ss = _dot(hi, e) + _dot(lo, e)
    r = lax.rsqrt(ss * (1.0 / head_dim) + EPS)
    rh, rl = _split2(r)
    return x * (_dot(rh, et) + _dot(rl, et)) * g_tiled


def _proj_head_rms(x, w_ref, g_ref, e_ref, et_ref, head_dim):
    width = w_ref.shape[1] // PROJ_SPLITS
    cols = [slice(i * width, (i + 1) * width) for i in range(PROJ_SPLITS)]
    ys = [_dot(x, w_ref[:, c]) for c in cols]
    return jnp.concatenate([_head_rms(y, g_ref[:, c], e_ref[c, :], et_ref[:, c], head_dim)
                            for y, c in zip(ys, cols)], axis=1)


def _const_spec(shape):
    return pl.BlockSpec(shape, lambda *_: (0,) * len(shape), pipeline_mode=pl.Buffered(1))


def _layer_spec(shape, layer):
    return pl.BlockSpec((None,) + tuple(shape), lambda *_: (layer,) + (0,) * len(shape),
                        pipeline_mode=pl.Buffered(1))


def _params(*sem):
    return pltpu.CompilerParams(dimension_semantics=sem, vmem_limit_bytes=VMEM_LIMIT)


def _pool_prompt_kernel(x_ref, halo_ref, g_ref, w_ref, sc_ref, o_ref, st_ref, *, tb, gdim):
    i = pl.program_id(1)
    g = g_ref[...]
    x = x_ref[...]
    xn = _rms(x, g)
    hn = _rms(halo_ref[...], g)
    sums = jnp.concatenate([jnp.where(i > 0, hn, 0.0), xn], axis=0)
    width, windows = 1, []
    for gi, w in enumerate(POOL_WINDOWS):
        while width < w:
            sums = sums + pltpu.roll(sums, width, 0)
            width *= 2
        assert width == w
        windows.append(sums[HALO:, :gdim])
        sums = sums[:, gdim:]
    pos = i * tb + lax.broadcasted_iota(jnp.int32, (tb, 1), 0)
    sc = sc_ref[...]
    for gi, w in enumerate(POOL_WINDOWS):
        sl = slice(gi * gdim, (gi + 1) * gdim)
        cnt = jnp.minimum(pos + 1, w).astype(F32)
        pooled = windows[gi] / cnt - xn[:, sl]
        mixed = _dot(pooled.astype(BF16), w_ref[gi]) * sc[:, sl]
        o_ref[:, sl] = x[:, sl] + mixed
    st_ref[...] = xn[tb - HALO:, :]


def _pool_prompt(h, g, w, layer, sc):
    B, T, D = h.shape
    tb = TOK_BLOCK
    gdim = D // len(POOL_WINDOWS)
    r = tb // HALO
    return pl.pallas_call(
        functools.partial(_pool_prompt_kernel, tb=tb, gdim=gdim),
        grid=(B, T // tb),
        in_specs=[
            pl.BlockSpec((None, tb, D), lambda b, i: (b, i, 0)),
            pl.BlockSpec((None, HALO, D), lambda b, i: (b, jnp.maximum(i * r - 1, 0), 0)),
            _const_spec((1, D)),
            _layer_spec((len(POOL_WINDOWS), gdim, gdim), layer),
            _const_spec((1, D)),
        ],
        out_specs=[
            pl.BlockSpec((None, tb, D), lambda b, i: (b, i, 0)),
            pl.BlockSpec((None, HALO, D), lambda b, i: (b, 0, 0)),
        ],
        out_shape=[jax.ShapeDtypeStruct((B, T, D), F32), jax.ShapeDtypeStruct((B, HALO, D), F32)],
        compiler_params=_params("arbitrary", "arbitrary"),
        name="pool_prompt",
    )(h, h, g, w, sc)


def _pool_sample_kernel(x_ref, pf_ref, g_ref, w_ref, sc_ref, o_ref, st_ref, *, ts, bb, gdim, cnts):
    g = g_ref[...]
    sc = sc_ref[...]
    xs = [x_ref[t] for t in range(ts)]
    xn = [_rms(x, g) for x in xs]
    ext = [pf_ref[r] for r in range(POOL_BUF)] + xn
    for gi, w in enumerate(POOL_WINDOWS):
        sl = slice(gi * gdim, (gi + 1) * gdim)
        pooled = []
        for t in range(ts):
            acc = ext[POOL_BUF + t][:, sl]
            for j in range(1, w):
                acc = acc + ext[POOL_BUF + t - j][:, sl]
            pooled.append(acc / cnts[t][gi] - xn[t][:, sl])
        mixed = _dot(jnp.concatenate(pooled, axis=0).astype(BF16), w_ref[gi]) * sc[:, sl]
        for t in range(ts):
            o_ref[t, :, sl] = xs[t][:, sl] + mixed[t * bb:(t + 1) * bb]
    for r in range(POOL_BUF):
        st_ref[r] = ext[ts + r]


def _pool_sample(h_tm, prefix_tm, g, w, layer, sc, past):
    ts, bs, D = h_tm.shape
    bb = 32
    gdim = D // len(POOL_WINDOWS)
    cnts = tuple(tuple(float(min(past + t + 1, w)) for w in POOL_WINDOWS) for t in range(ts))
    return pl.pallas_call(
        functools.partial(_pool_sample_kernel, ts=ts, bb=bb, gdim=gdim, cnts=cnts),
        grid=(bs // bb,),
        in_specs=[
            pl.BlockSpec((ts, bb, D), lambda i: (0, i, 0)),
            pl.BlockSpec((POOL_BUF, bb, D), lambda i: (0, i, 0)),
            _const_spec((1, D)),
            _layer_spec((len(POOL_WINDOWS), gdim, gdim), layer),
            _const_spec((1, D)),
        ],
        out_specs=[
            pl.BlockSpec((ts, bb, D), lambda i: (0, i, 0)),
            pl.BlockSpec((POOL_BUF, bb, D), lambda i: (0, i, 0)),
        ],
        out_shape=[jax.ShapeDtypeStruct((ts, bs, D), F32), jax.ShapeDtypeStruct((POOL_BUF, bs, D), F32)],
        compiler_params=_params("arbitrary"),
        name="pool_sample",
    )(h_tm, prefix_tm, g, w, sc)


def _mlp_kernel(*refs, o_mode, n_chunks):
    if o_mode is None:
        h_ref, g_ref, wu_ref, wd_ref, out_ref = refs
        h = h_ref[...]
    else:
        h_ref, o_ref, wo_ref, g_ref, wu_ref, wd_ref, out_ref = refs
        o = o_ref[...]
        if o_mode == "transposed":
            proj = lax.dot_general(o, wo_ref[...], (((0,), (0,)), ((), ())), preferred_element_type=F32)
        else:
            proj = _dot(o, wo_ref[...])
        h = h_ref[...] + proj
    xn = _rms(h, g_ref[...]).astype(BF16)
    acc = h
    for c in range(n_chunks):
        cs = slice(c * FF_CHUNK, (c + 1) * FF_CHUNK)
        a = jnp.maximum(_dot(xn, wu_ref[:, cs]), 0.0)
        acc = acc + _dot((a * a).astype(BF16), wd_ref[cs, :])
    out_ref[...] = acc


def _mlp(h, g, w_up, w_down, layer, o=None, w_o=None, o_layer=None, o_transposed=False):
    N, D = h.shape
    F = w_up.shape[2]
    tm = TOK_BLOCK
    tok = pl.BlockSpec((tm, D), lambda i: (i, 0))
    in_specs = [tok]
    args = [h]
    o_mode = None
    if o is not None:
        if o_transposed:
            o_mode = "transposed"
            nb = o.shape[2] // tm
            in_specs.append(pl.BlockSpec((None, D, tm), lambda i: (i // nb, 0, i % nb)))
        else:
            o_mode = "rows"
            in_specs.append(tok)
        in_specs.append(_layer_spec((D, D), o_layer))
        args += [o, w_o]
    in_specs += [_const_spec((1, D)), _layer_spec((D, F), layer), _layer_spec((F, D), layer)]
    args += [g, w_up, w_down]
    return pl.pallas_call(
        functools.partial(_mlp_kernel, o_mode=o_mode, n_chunks=F // FF_CHUNK),
        grid=(N // tm,),
        in_specs=in_specs,
        out_specs=tok,
        out_shape=jax.ShapeDtypeStruct((N, D), F32),
        compiler_params=_params("arbitrary"),
        name="mlp" if o is None else "mlp_o",
    )(*args)


def _log_sigmoid(z):
    return -(jnp.maximum(-z, 0.0) + jnp.log1p(jnp.exp(-jnp.abs(z))))


def _kv_core(h_ref, g_ref, wk_ref, wv_ref, gk_ref, wf_ref, bf_ref, e_ref, et_ref, head_dim, v_transposed):
    hn = _rms(h_ref[...], g_ref[...]).astype(BF16)
    k = _proj_head_rms(hn, wk_ref, gk_ref, e_ref, et_ref, head_dim)
    v = _dot_nt(wv_ref[...], hn) if v_transposed else _dot(hn, wv_ref[...])
    lf = _log_sigmoid(_dot(hn, wf_ref[...]) + bf_ref[...])
    return k, v, lf


def _kv_sample_kernel(h_ref, g_ref, wk_ref, wv_ref, gk_ref, wf_ref, bf_ref, e_ref, et_ref,
                      k_ref, v_ref, lf_ref, kb_ref, vb_ref, *, head_dim):
    k, v, lf = _kv_core(h_ref, g_ref, wk_ref, wv_ref, gk_ref, wf_ref, bf_ref, e_ref, et_ref, head_dim, False)
    k_ref[...] = k
    v_ref[...] = v
    kb_ref[...] = k.astype(BF16)
    vb_ref[...] = v.astype(BF16)
    lf_ref[...] = lf[:, :N_HEADS]


def _kv_sample(h, g, wk, wv, gk_t, wf_p, bf_p, e, et):
    N, D = h.shape
    tm = TOK_BLOCK
    tok = pl.BlockSpec((tm, D), lambda i: (i, 0))
    return pl.pallas_call(
        functools.partial(_kv_sample_kernel, head_dim=D // N_HEADS),
        grid=(N // tm,),
        in_specs=[tok, _const_spec((1, D)), _const_spec((D, D)), _const_spec((D, D)), _const_spec((1, D)),
                  _const_spec((D, LANES)), _const_spec((1, LANES)), _const_spec((D, LANES)),
                  _const_spec((LANES, D))],
        out_specs=[tok, tok, pl.BlockSpec((tm, N_HEADS), lambda i: (i, 0)), tok, tok],
        out_shape=[jax.ShapeDtypeStruct((N, D), F32), jax.ShapeDtypeStruct((N, D), F32),
                   jax.ShapeDtypeStruct((N, N_HEADS), F32),
                   jax.ShapeDtypeStruct((N, D), BF16), jax.ShapeDtypeStruct((N, D), BF16)],
        compiler_params=_params("arbitrary"),
        name="shared_kv_sample",
    )(h, g, wk, wv, gk_t, wf_p, bf_p, e, et)


def _kv_prompt_kernel(h_ref, g_ref, wk_ref, wv_ref, gk_ref, wf_ref, bf_ref, e_ref, et_ref, tril_ref, spread_ref,
                      kt_ref, vt_ref, lft_ref, ct_ref, ka_ref, va_ref, carry_ref, *, nb, head_dim):
    i = pl.program_id(0)
    k, v_t, lf = _kv_core(h_ref, g_ref, wk_ref, wv_ref, gk_ref, wf_ref, bf_ref, e_ref, et_ref, head_dim, True)
    tm = k.shape[0]
    lane = lax.broadcasted_iota(jnp.int32, (tm, LANES), 1)
    lf = jnp.where(lane < N_HEADS, lf, 0.0)
    kt_ref[...] = k.T
    vt_ref[...] = v_t
    lft_ref[...] = lf.T[:N_HEADS]

    @pl.when(i % nb == 0)
    def _():
        carry_ref[...] = jnp.zeros_like(carry_ref)

    c = carry_ref[...]
    for part in _split3(lf):
        c = c + _dot(tril_ref[...], part)
    carry_ref[...] = c[tm - 1:, :]
    ct_ref[...] = c.T[:N_HEADS]

    hi, mid, lo = (t.astype(F32) for t in _split3(c * -LOG2E))
    terms = hi + pltpu.roll(mid, N_HEADS, 1) + pltpu.roll(lo, 2 * N_HEADS, 1)
    ones = jnp.where((lane >= head_dim + 3) & (lane < head_dim + 6), 1.0, 0.0)
    aug = _dot(terms.astype(BF16), spread_ref[...])
    low = lane < head_dim
    for p in range(k.shape[1] // LANES):
        kp = k[:, p * LANES:(p + 1) * LANES]
        even = jnp.where(low, kp, ones) + aug[:, (2 * p) * LANES:(2 * p + 1) * LANES]
        odd = jnp.where(low, pltpu.roll(kp, head_dim, 1), ones) + aug[:, (2 * p + 1) * LANES:(2 * p + 2) * LANES]
        ka_ref[:, (2 * p) * LANES:(2 * p + 1) * LANES] = even.astype(BF16)
        ka_ref[:, (2 * p + 1) * LANES:(2 * p + 2) * LANES] = odd.astype(BF16)

    va_ref[:, 0:head_dim, :] = v_t.reshape(N_HEADS, head_dim, tm).astype(BF16)
    sub = lax.broadcasted_iota(jnp.int32, (N_HEADS, V_EXTRA, tm), 1)
    va_ref[:, head_dim:, :] = jnp.where(sub == 0, 1.0, 0.0).astype(BF16)


def _kv_prompt(h, T, g, wk, wv_t, gk_t, wf_p, bf_p, e, et):
    N, D = h.shape
    B = N // T
    head_dim = D // N_HEADS
    tm = TOK_BLOCK
    nb = T // tm
    tok = pl.BlockSpec((tm, D), lambda i: (i, 0))
    tril = (jnp.arange(tm)[:, None] >= jnp.arange(tm)[None, :]).astype(BF16)
    src = jnp.arange(LANES)[:, None]
    dst = jnp.arange(N_HEADS * LANES)[None, :]
    spread = ((src < 3 * N_HEADS) & (dst // LANES == src % N_HEADS)
              & (dst % LANES == head_dim + src // N_HEADS)).astype(BF16)
    seq_t = lambda r: pl.BlockSpec((None, r, tm), lambda i: (i // nb, 0, i % nb))
    return pl.pallas_call(
        functools.partial(_kv_prompt_kernel, nb=nb, head_dim=head_dim),
        grid=(N // tm,),
        in_specs=[tok, _const_spec((1, D)), _const_spec((D, D)), _const_spec((D, D)), _const_spec((1, D)),
                  _const_spec((D, LANES)), _const_spec((1, LANES)), _const_spec((D, LANES)),
                  _const_spec((LANES, D)), _const_spec((tm, tm)), _const_spec((LANES, N_HEADS * LANES))],
        out_specs=[seq_t(D), seq_t(D), seq_t(N_HEADS), seq_t(N_HEADS),
                   pl.BlockSpec((tm, N_HEADS * LANES), lambda i: (i, 0)),
                   pl.BlockSpec((None, N_HEADS, head_dim + V_EXTRA, tm), lambda i: (i // nb, 0, 0, i % nb))],
        out_shape=[jax.ShapeDtypeStruct((B, D, T), F32), jax.ShapeDtypeStruct((B, D, T), F32),
                   jax.ShapeDtypeStruct((B, N_HEADS, T), F32), jax.ShapeDtypeStruct((B, N_HEADS, T), F32),
                   jax.ShapeDtypeStruct((N, N_HEADS * LANES), BF16),
                   jax.ShapeDtypeStruct((B, N_HEADS, head_dim + V_EXTRA, T), BF16)],
        scratch_shapes=[pltpu.VMEM((1, LANES), F32)],
        compiler_params=_params("arbitrary"),
        name="shared_kv_prompt",
    )(h, g, wk, wv_t, gk_t, wf_p, bf_p, e, et, tril, spread)


def _q_kernel(*refs, head_dim, transposed):
    if transposed:
        h_ref, g_ref, wq_ref, gq_ref, e_ref, et_ref, ct_ref, bound_ref, q_ref = refs
    else:
        h_ref, g_ref, wq_ref, gq_ref, e_ref, et_ref, q_ref = refs
    xn = _rms(h_ref[...], g_ref[...]).astype(BF16)
    q = _proj_head_rms(xn, wq_ref, gq_ref, e_ref, et_ref, head_dim)
    if not transposed:
        q_ref[...] = (q * (head_dim ** -0.5)).astype(BF16)
        return
    tm = q.shape[0]
    q_t = (q * (head_dim ** -0.5 * LOG2E)).T.reshape(N_HEADS, head_dim, tm)
    q_ref[:, 0:head_dim, :] = q_t.astype(BF16)
    shape = (N_HEADS, LANES - head_dim, tm)
    sub = lax.broadcasted_iota(jnp.int32, shape, 1)
    hi, mid, lo = (jnp.broadcast_to(t.astype(F32)[:, None, :], shape)
                   for t in _split3(ct_ref[...] * LOG2E - bound_ref[...]))
    tail = jnp.where(sub < 3, 1.0, jnp.where(sub == 3, hi, jnp.where(sub == 4, mid, jnp.where(sub == 5, lo, 0.0))))
    q_ref[:, head_dim:, :] = tail.astype(BF16)


def _qproj(h, g, wq, layer, gq_t, e, et, ct=None, bound=None):
    N, D = h.shape
    tm = TOK_BLOCK
    tok = pl.BlockSpec((tm, D), lambda i: (i, 0))
    in_specs = [tok, _const_spec((1, D)), _layer_spec((D, D), layer), _const_spec((1, D)),
                _const_spec((D, LANES)), _const_spec((LANES, D))]
    args = [h, g, wq, gq_t, e, et]
    if ct is None:
        out_spec, out_shape = tok, jax.ShapeDtypeStruct((N, D), BF16)
    else:
        T = ct.shape[2]
        nb = T // tm
        in_specs += [pl.BlockSpec((None, N_HEADS, tm), lambda i: (i // nb, 0, i % nb)), _const_spec((1, 1))]
        args += [ct, bound]
        out_spec = pl.BlockSpec((None, N_HEADS, LANES, tm), lambda i: (i // nb, 0, 0, i % nb))
        out_shape = jax.ShapeDtypeStruct((N // T, N_HEADS, LANES, T), BF16)
    return pl.pallas_call(
        functools.partial(_q_kernel, head_dim=D // N_HEADS, transposed=ct is not None),
        grid=(N // tm,),
        in_specs=in_specs,
        out_specs=out_spec,
        out_shape=out_shape,
        compiler_params=_params("arbitrary"),
        name="q_proj" if ct is None else "q_proj_t",
    )(*args)


def _attn_prompt_kernel(q_ref, k_ref, v_ref, o_ref, *, blk, head_dim, bounded):
    qi = pl.program_id(2)
    n_h = q_ref.shape[0]
    q_t = [q_ref[x] for x in range(n_h)]
    key_iota = lax.broadcasted_iota(jnp.int32, (blk, blk), 0)
    qry_iota = lax.broadcasted_iota(jnp.int32, (blk, blk), 1)

    def step(j0, n_blocks, carry, diagonal_last):
        offs = [pl.multiple_of((j0 + b) * blk, blk) for b in range(n_blocks)]
        s_all = [[_dot(k_ref[pl.ds(off, blk), x * LANES:(x + 1) * LANES], q_t[x]) for x in range(n_h)]
                 for off in offs]
        out = []
        for x, (m, acc) in enumerate(carry):
            for b, off in enumerate(offs):
                s = s_all[b][x]
                if diagonal_last and b == n_blocks - 1:
                    s = jnp.where(key_iota <= qry_iota, s, NEG_BIG)
                if bounded:
                    acc = acc + _dot(v_ref[x, :, pl.ds(off, blk)], jnp.exp2(s).astype(BF16))
                    continue
                m_new = jnp.maximum(m, jnp.max(s, axis=0, keepdims=True))
                p = jnp.exp2(s - m_new).astype(BF16)
                acc = jnp.exp2(m - m_new) * acc + _dot(v_ref[x, :, pl.ds(off, blk)], p)
                m = m_new
            out.append((m, acc))
        return tuple(out)

    init = tuple((jnp.full((1, blk), NEG_BIG, F32), jnp.zeros((v_ref.shape[1], blk), F32)) for _ in range(n_h))
    n_wide = qi // ATT_UNROLL
    carry = lax.fori_loop(0, n_wide, lambda jj, c: step(jj * ATT_UNROLL, ATT_UNROLL, c, False), init)
    j0 = n_wide * ATT_UNROLL
    tails = [functools.partial(step, j0, r + 1, diagonal_last=True) for r in range(ATT_UNROLL)]
    for x, (_, acc) in enumerate(lax.switch(qi - j0, tails, carry)):
        o_ref[x] = (acc[:head_dim] / acc[head_dim:head_dim + 1]).astype(o_ref.dtype)


def _attn_prompt(q_t, k_aug, v_aug, bounded):
    B, H, _, T = q_t.shape
    vr = v_aug.shape[2]
    head_dim = vr - V_EXTRA
    blk = ATT_BLOCK
    hb = ATT_HEADS
    return pl.pallas_call(
        functools.partial(_attn_prompt_kernel, blk=blk, head_dim=head_dim, bounded=bounded),
        grid=(B, H // hb, T // blk),
        in_specs=[
            pl.BlockSpec((None, hb, LANES, blk), lambda b, h, i: (b, h, 0, i)),
            pl.BlockSpec((None, T, hb * LANES), lambda b, h, i: (b, 0, h)),
            pl.BlockSpec((None, hb, vr, T), lambda b, h, i: (b, h, 0, 0)),
        ],
        out_specs=pl.BlockSpec((None, hb, head_dim, blk), lambda b, h, i: (b, h, 0, i)),
        out_shape=jax.ShapeDtypeStruct((B, H, head_dim, T), BF16),
        compiler_params=_params("arbitrary", "arbitrary", "arbitrary"),
        name="attn_prompt_bounded" if bounded else "attn_prompt",
    )(q_t, k_aug, v_aug)


def _attn_sample_kernel(pt_ref, q_ref, lfn_ref, kn_ref, vn_ref, *refs, n_pages, page, ts, head_dim, new_pad):
    del pt_ref
    lf_refs = refs[:n_pages]
    k_refs = refs[n_pages:2 * n_pages]
    v_refs = refs[2 * n_pages:3 * n_pages]
    o_ref = refs[3 * n_pages]
    D = q_ref.shape[1]
    n_pairs = D // LANES
    tp = 8
    lane = lax.broadcasted_iota(jnp.int32, (tp, LANES), 1)
    first = lane < head_dim
    tri = (lax.broadcasted_iota(jnp.int32, (page, page), 0)
           <= lax.broadcasted_iota(jnp.int32, (page, page), 1)).astype(BF16)

    q = q_ref[...].astype(F32)
    q8 = jnp.concatenate([q, jnp.zeros((tp - ts, D), F32)], axis=0)
    q_pairs = []
    for p in range(n_pairs):
        qp = q8[:, p * LANES:(p + 1) * LANES]
        q_pairs.append(jnp.concatenate([jnp.where(first, qp, 0.0), jnp.where(first, 0.0, qp)], axis=0).astype(BF16))

    def running_sum(lf_t, carry):
        parts = _dot(jnp.concatenate(_split3(lf_t), axis=0), tri)
        return parts[:N_HEADS] + parts[N_HEADS:2 * N_HEADS] + parts[2 * N_HEADS:] + carry

    def head_rows(c):
        n = c.shape[1]
        return jnp.broadcast_to(c[:, None, :], (N_HEADS, tp, n)).reshape(N_HEADS * tp, n)

    carry = jnp.zeros((N_HEADS, 1), F32)
    s_pages = []
    for j in range(n_pages):
        c = running_sum(lf_refs[j][...], carry)
        carry = c[:, page - 1:]
        s = [_dot(q_pairs[p], k_refs[j][p * LANES:(p + 1) * LANES, :].astype(BF16)) for p in range(n_pairs)]
        s_pages.append(jnp.concatenate(s, axis=0) - head_rows(c))
    c = running_sum(lfn_ref[...], carry)[:, :new_pad]
    s = [_dot_nt(q_pairs[p], kn_ref[:, p * LANES:(p + 1) * LANES]) for p in range(n_pairs)]
    t_row = lax.broadcasted_iota(jnp.int32, (N_HEADS * tp, new_pad), 0) % tp
    t_key = lax.broadcasted_iota(jnp.int32, (N_HEADS * tp, new_pad), 1)
    s_new = jnp.where(t_key <= t_row, jnp.concatenate(s, axis=0) - head_rows(c), NEG_BIG)

    m_el = s_pages[0]
    for s in s_pages[1:]:
        m_el = jnp.maximum(m_el, s)
    m = jnp.maximum(jnp.max(m_el, axis=1, keepdims=True), jnp.max(s_new, axis=1, keepdims=True))
    l_el = jnp.zeros((N_HEADS * tp, page), F32)
    acc_t = jnp.zeros((D, N_HEADS * tp), F32)
    for j in range(n_pages):
        pj = jnp.exp(s_pages[j] - m)
        l_el = l_el + pj
        acc_t = acc_t + _dot(v_refs[j][...].astype(BF16), pj.T.astype(BF16))
    p_new = jnp.exp(s_new - m)
    l = jnp.sum(l_el, axis=1, keepdims=True) + jnp.sum(p_new, axis=1, keepdims=True)
    o = (acc_t.T + _dot(p_new.astype(BF16), vn_ref[...])) / l
    row_head = lax.broadcasted_iota(jnp.int32, (N_HEADS * tp, D), 0) // tp
    col_head = lax.broadcasted_iota(jnp.int32, (N_HEADS * tp, D), 1) // head_dim
    own = jnp.where(row_head == col_head, o, 0.0).reshape(N_HEADS, tp, D)
    o_ref[...] = jnp.sum(own, axis=0)[:ts].astype(o_ref.dtype)


def _attn_sample(q, lf_new_t, k_new, v_new, cache_k, cache_v, cache_logf, page_table):
    bs, ts, D = q.shape
    n_pages = page_table.shape[1]
    n_pool, page = cache_k.shape[0], cache_k.shape[1]
    assert page == LANES
    new_pad = k_new.shape[1]
    ck = jnp.transpose(cache_k, (0, 2, 3, 1)).reshape(n_pool, D, page)
    cv = jnp.transpose(cache_v, (0, 2, 3, 1)).reshape(n_pool, D, page)
    cl = jnp.transpose(cache_logf, (0, 2, 1))
    seq = lambda n: pl.BlockSpec((None, n, D), lambda b, pt: (b, 0, 0))
    paged = lambda r: [pl.BlockSpec((None, r, page), lambda b, pt, j=j: (pt[b, j], 0, 0)) for j in range(n_pages)]
    return pl.pallas_call(
        functools.partial(_attn_sample_kernel, n_pages=n_pages, page=page, ts=ts,
                          head_dim=D // N_HEADS, new_pad=new_pad),
        grid_spec=pltpu.PrefetchScalarGridSpec(
            num_scalar_prefetch=1,
            grid=(bs,),
            in_specs=[seq(ts), pl.BlockSpec((None, N_HEADS, page), lambda b, pt: (b, 0, 0)),
                      seq(new_pad), seq(new_pad)] + paged(N_HEADS) + paged(D) + paged(D),
            out_specs=seq(ts),
        ),
        out_shape=jax.ShapeDtypeStruct((bs, ts, D), BF16),
        compiler_params=_params("arbitrary"),
        name="attn_sample",
    )(page_table, q, lf_new_t, k_new, v_new, *([cl] * n_pages), *([ck] * n_pages), *([cv] * n_pages))


def kernel(x_prompt, x_sample, state_pool, cache_k, cache_v, cache_logf, page_table, g_pool, w_pool, pool_scale,
           g_attn, w_q, g_q, w_o, g_kv, w_k, w_v, g_k, w_f, b_f, g_mlp, w_up, w_down):
    Bp, Tp, D = x_prompt.shape
    Bs, Ts, _ = x_sample.shape
    n_a = g_pool.shape[0]
    n_b = g_attn.shape[0]
    head_dim = D // N_HEADS
    past = page_table.shape[1] * cache_k.shape[1]
    new_pad = 16

    row = lambda a: a.reshape(1, -1).astype(F32)
    w_pool_b, w_up_b, w_down_b, w_q_b, w_o_b = (w.astype(BF16) for w in (w_pool, w_up, w_down, w_q, w_o))
    w_k_b, w_v_b = w_k.astype(BF16), w_v.astype(BF16)
    w_f_p = jnp.pad(w_f, ((0, 0), (0, LANES - N_HEADS))).astype(BF16)
    b_f_p = jnp.pad(b_f.astype(F32), (0, LANES - N_HEADS)).reshape(1, LANES)
    e = (jnp.arange(D)[:, None] // head_dim == jnp.arange(LANES)[None, :]).astype(BF16)
    et = e.T
    g_k_t = row(jnp.tile(g_k, N_HEADS))

    h_p = x_prompt
    h_s = jnp.transpose(x_sample, (1, 0, 2))
    st_p, st_s = [], []
    for l in range(n_a):
        args = (row(g_pool[l]), w_pool_b, l, row(pool_scale[l]))
        h_p, tail = _pool_prompt(h_p, *args)
        st_p.append(tail[:, HALO - POOL_BUF:])
        h_p = _mlp(h_p.reshape(Bp * Tp, D), row(g_mlp[l]), w_up_b, w_down_b, l).reshape(Bp, Tp, D)
        h_s, st = _pool_sample(h_s, jnp.transpose(state_pool[l], (1, 0, 2)), *args, past)
        st_s.append(jnp.transpose(st, (1, 0, 2)))
        h_s = _mlp(h_s.reshape(Ts * Bs, D), row(g_mlp[l]), w_up_b, w_down_b, l).reshape(Ts, Bs, D)
    pool_state_prompt = jnp.stack(st_p, axis=0)
    pool_state_sample = jnp.stack(st_s, axis=0)

    h_p = h_p.reshape(Bp * Tp, D)
    h_s = jnp.transpose(h_s, (1, 0, 2)).reshape(Bs * Ts, D)

    kv_tail = (g_k_t, w_f_p, b_f_p, e, et)
    kt_p, vt_p, lft_p, ct_p, ka_p, va_p = _kv_prompt(h_p, Tp, row(g_kv), w_k_b, w_v_b.T, *kv_tail)
    k_s, v_s, lf_s, kb_s, vb_s = _kv_sample(h_s, row(g_kv), w_k_b, w_v_b, *kv_tail)
    pad_new = lambda a, n: jnp.pad(a.reshape(Bs, Ts, -1), ((0, 0), (0, n - Ts), (0, 0)))
    lfn_t = jnp.transpose(pad_new(lf_s, cache_logf.shape[1]), (0, 2, 1))
    kn, vn = pad_new(kb_s, new_pad), pad_new(vb_s, new_pad)
    ka_p = ka_p.reshape(Bp, Tp, N_HEADS * LANES)

    for l in range(n_b):
        j = n_a + l
        g_q_t = row(jnp.tile(g_q[l], N_HEADS))
        bound = (1.02 * LOG2E * head_dim ** 0.5) * jnp.max(jnp.abs(g_q[l])) * jnp.max(jnp.abs(g_k))
        q_p = _qproj(h_p, row(g_attn[l]), w_q_b, l, g_q_t, e, et, ct_p, bound.reshape(1, 1).astype(F32))
        o_p = lax.cond(2.0 * bound < MAX_SELF_DEFICIT,
                       functools.partial(_attn_prompt, bounded=True),
                       functools.partial(_attn_prompt, bounded=False), q_p, ka_p, va_p).reshape(Bp, D, Tp)
        h_p = _mlp(h_p, row(g_mlp[j]), w_up_b, w_down_b, j, o_p, w_o_b, l, o_transposed=True)
        q_s = _qproj(h_s, row(g_attn[l]), w_q_b, l, g_q_t, e, et)
        o_s = _attn_sample(q_s.reshape(Bs, Ts, D), lfn_t, kn, vn, cache_k, cache_v, cache_logf, page_table)
        h_s = _mlp(h_s, row(g_mlp[j]), w_up_b, w_down_b, j, o_s.reshape(Bs * Ts, D), w_o_b, l)

    hd_t = lambda a: jnp.transpose(a.reshape(Bp, N_HEADS, head_dim, Tp), (0, 3, 1, 2))
    hd4 = lambda a: a.reshape(Bs, Ts, N_HEADS, head_dim)
    return (h_p.reshape(Bp, Tp, D), h_s.reshape(Bs, Ts, D), pool_state_prompt, pool_state_sample,
            hd_t(kt_p), hd_t(vt_p), jnp.transpose(lft_p, (0, 2, 1)).astype(cache_logf.dtype),
            hd4(k_s), hd4(v_s), lf_s.reshape(Bs, Ts, N_HEADS).astype(cache_logf.dtype))
```

```python
import functools

import jax
import jax.numpy as jnp
from jax import lax
from jax.experimental import pallas as pl
from jax.experimental.pallas import tpu as pltpu

F32 = jnp.float32
BF16 = jnp.bfloat16

EPS = 1e-6
POOL_WINDOWS = (2, 4, 8, 16)
POOL_BUF = max(POOL_WINDOWS) - 1
assert list(POOL_WINDOWS) == sorted(POOL_WINDOWS) and all(w & (w - 1) == 0 for w in POOL_WINDOWS)
N_HEADS = 16
LANES = 128
HALO = 16
VMEM_LIMIT = 56 * 1024 * 1024
NEG_BIG = -1e30

TOK_BLOCK = 512
FF_CHUNK = 1024
PROJ_SPLITS = 4
ATT_BLOCK = 512
ATT_UNROLL = 4
ATT_HEADS = 4
V_EXTRA = 16
LOG2E = 1.4426950408889634
MAX_SELF_DEFICIT = 120.0


def _dot(a, b):
    return jnp.dot(a, b, preferred_element_type=F32)


def _dot_nt(a, b):
    return lax.dot_general(a, b, (((1,), (1,)), ((), ())), preferred_element_type=F32)


def _rms(x, g):
    ms = jnp.mean(x * x, axis=-1, keepdims=True)
    return x * lax.rsqrt(ms + EPS) * g


def _split2(x):
    hi = x.astype(BF16)
    lo = (x - hi.astype(F32)).astype(BF16)
    return hi, lo


def _split3(x):
    hi = x.astype(BF16)
    r1 = x - hi.astype(F32)
    mid = r1.astype(BF16)
    lo = (r1 - mid.astype(F32)).astype(BF16)
    return hi, mid, lo


def _head_rms(x, g_tiled, e, et, head_dim):
    hi, lo = _split2(x * x)
    ss = _dot(hi, e) + _dot(lo, e)
    r = lax.rsqrt(ss * (1.0 / head_dim) + EPS)
    rh, rl = _split2(r)
    return x * (_dot(rh, et) + _dot(rl, et)) * g_tiled


def _proj_head_rms(x, w_ref, g_ref, e_ref, et_ref, head_dim):
    width = w_ref.shape[1] // PROJ_SPLITS
    cols = [slice(i * width, (i + 1) * width) for i in range(PROJ_SPLITS)]
    ys = [_dot(x, w_ref[:, c]) for c in cols]
    return jnp.concatenate([_head_rms(y, g_ref[:, c], e_ref[c, :], et_ref[:, c], head_dim)
                            for y, c in zip(ys, cols)], axis=1)


def _const_spec(shape):
    return pl.BlockSpec(shape, lambda *_: (0,) * len(shape), pipeline_mode=pl.Buffered(1))


def _layer_spec(shape, layer):
    return pl.BlockSpec((None,) + tuple(shape), lambda *_: (layer,) + (0,) * len(shape),
                        pipeline_mode=pl.Buffered(1))


def _params(*sem):
    return pltpu.CompilerParams(dimension_semantics=sem, vmem_limit_bytes=VMEM_LIMIT)


def _pool_prompt_kernel(x_ref, halo_ref, g_ref, w_ref, sc_ref, o_ref, st_ref, *, tb, gdim):
    i = pl.program_id(1)
    g = g_ref[...]
    x = x_ref[...]
    xn = _rms(x, g)
    hn = _rms(halo_ref[...], g)
    sums = jnp.concatenate([jnp.where(i > 0, hn, 0.0), xn], axis=0)
    width, windows = 1, []
    for w in POOL_WINDOWS:
        while width < w:
            sums = sums + pltpu.roll(sums, width, 0)
            width *= 2
        windows.append(sums[HALO:, :gdim])
        sums = sums[:, gdim:]
    pos = i * tb + lax.broadcasted_iota(jnp.int32, (tb, 1), 0)
    sc = sc_ref[...]
    for gi, w in enumerate(POOL_WINDOWS):
        sl = slice(gi * gdim, (gi + 1) * gdim)
        cnt = jnp.minimum(pos + 1, w).astype(F32)
        pooled = windows[gi] / cnt - xn[:, sl]
        mixed = _dot(pooled.astype(BF16), w_ref[gi]) * sc[:, sl]
        o_ref[:, sl] = x[:, sl] + mixed
    st_ref[...] = xn[tb - HALO:, :]


def _pool_prompt(h, g, w, layer, sc):
    B, T, D = h.shape
    tb = TOK_BLOCK
    gdim = D // len(POOL_WINDOWS)
    r = tb // HALO
    return pl.pallas_call(
        functools.partial(_pool_prompt_kernel, tb=tb, gdim=gdim),
        grid=(B, T // tb),
        in_specs=[
            pl.BlockSpec((None, tb, D), lambda b, i: (b, i, 0)),
            pl.BlockSpec((None, HALO, D), lambda b, i: (b, jnp.maximum(i * r - 1, 0), 0)),
            _const_spec((1, D)),
            _layer_spec((len(POOL_WINDOWS), gdim, gdim), layer),
            _const_spec((1, D)),
        ],
        out_specs=[
            pl.BlockSpec((None, tb, D), lambda b, i: (b, i, 0)),
            pl.BlockSpec((None, HALO, D), lambda b, i: (b, 0, 0)),
        ],
        out_shape=[jax.ShapeDtypeStruct((B, T, D), F32), jax.ShapeDtypeStruct((B, HALO, D), F32)],
        compiler_params=_params("arbitrary", "arbitrary"),
        name="pool_prompt",
    )(h, h, g, w, sc)


def _pool_sample_kernel(x_ref, pf_ref, g_ref, w_ref, sc_ref, o_ref, st_ref, *, ts, bb, gdim, cnts):
    g = g_ref[...]
    sc = sc_ref[...]
    xs = [x_ref[t] for t in range(ts)]
    xn = [_rms(x, g) for x in xs]
    ext = [pf_ref[r] for r in range(POOL_BUF)] + xn
    for gi, w in enumerate(POOL_WINDOWS):
        sl = slice(gi * gdim, (gi + 1) * gdim)
        pooled = []
        for t in range(ts):
            acc = ext[POOL_BUF + t][:, sl]
            for j in range(1, w):
                acc = acc + ext[POOL_BUF + t - j][:, sl]
            pooled.append(acc / cnts[t][gi] - xn[t][:, sl])
        mixed = _dot(jnp.concatenate(pooled, axis=0).astype(BF16), w_ref[gi]) * sc[:, sl]
        for t in range(ts):
            o_ref[t, :, sl] = xs[t][:, sl] + mixed[t * bb:(t + 1) * bb]
    for r in range(POOL_BUF):
        st_ref[r] = ext[ts + r]


def _pool_sample(h_tm, prefix_tm, g, w, layer, sc, past):
    ts, bs, D = h_tm.shape
    bb = 32
    gdim = D // len(POOL_WINDOWS)
    cnts = tuple(tuple(float(min(past + t + 1, w)) for w in POOL_WINDOWS) for t in range(ts))
    return pl.pallas_call(
        functools.partial(_pool_sample_kernel, ts=ts, bb=bb, gdim=gdim, cnts=cnts),
        grid=(bs // bb,),
        in_specs=[
            pl.BlockSpec((ts, bb, D), lambda i: (0, i, 0)),
            pl.BlockSpec((POOL_BUF, bb, D), lambda i: (0, i, 0)),
            _const_spec((1, D)),
            _layer_spec((len(POOL_WINDOWS), gdim, gdim), layer),
            _const_spec((1, D)),
        ],
        out_specs=[
            pl.BlockSpec((ts, bb, D), lambda i: (0, i, 0)),
            pl.BlockSpec((POOL_BUF, bb, D), lambda i: (0, i, 0)),
        ],
        out_shape=[jax.ShapeDtypeStruct((ts, bs, D), F32), jax.ShapeDtypeStruct((POOL_BUF, bs, D), F32)],
        compiler_params=_params("arbitrary"),
        name="pool_sample",
    )(h_tm, prefix_tm, g, w, sc)


def _mlp_kernel(*refs, o_mode, n_chunks):
    if o_mode is None:
        h_ref, g_ref, wu_ref, wd_ref, out_ref = refs
        h = h_ref[...]
    else:
        h_ref, o_ref, wo_ref, g_ref, wu_ref, wd_ref, out_ref = refs
        o = o_ref[...]
        if o_mode == "transposed":
            proj = lax.dot_general(o, wo_ref[...], (((0,), (0,)), ((), ())), preferred_element_type=F32)
        else:
            proj = _dot(o, wo_ref[...])
        h = h_ref[...] + proj
    xn = _rms(h, g_ref[...]).astype(BF16)
    acc = h
    for c in range(n_chunks):
        cs = slice(c * FF_CHUNK, (c + 1) * FF_CHUNK)
        a = jnp.maximum(_dot(xn, wu_ref[:, cs]), 0.0)
        acc = acc + _dot((a * a).astype(BF16), wd_ref[cs, :])
    out_ref[...] = acc


def _mlp(h, g, w_up, w_down, layer, o=None, w_o=None, o_layer=None, o_transposed=False):
    N, D = h.shape
    F = w_up.shape[2]
    tm = TOK_BLOCK
    tok = pl.BlockSpec((tm, D), lambda i: (i, 0))
    in_specs = [tok]
    args = [h]
    o_mode = None
    if o is not None:
        if o_transposed:
            o_mode = "transposed"
            nb = o.shape[2] // tm
            in_specs.append(pl.BlockSpec((None, D, tm), lambda i: (i // nb, 0, i % nb)))
        else:
            o_mode = "rows"
            in_specs.append(tok)
        in_specs.append(_layer_spec((D, D), o_layer))
        args += [o, w_o]
    in_specs += [_const_spec((1, D)), _layer_spec((D, F), layer), _layer_spec((F, D), layer)]
    args += [g, w_up, w_down]
    return pl.pallas_call(
        functools.partial(_mlp_kernel, o_mode=o_mode, n_chunks=F // FF_CHUNK),
        grid=(N // tm,),
        in_specs=in_specs,
        out_specs=tok,
        out_shape=jax.ShapeDtypeStruct((N, D), F32),
        compiler_params=_params("arbitrary"),
        name="mlp" if o is None else "mlp_o",
    )(*args)


def _log_sigmoid(z):
    return -(jnp.maximum(-z, 0.0) + jnp.log1p(jnp.exp(-jnp.abs(z))))


def _kv_core(h_ref, g_ref, wk_ref, wv_ref, gk_ref, wf_ref, bf_ref, e_ref, et_ref, head_dim, v_transposed):
    hn = _rms(h_ref[...], g_ref[...]).astype(BF16)
    k = _proj_head_rms(hn, wk_ref, gk_ref, e_ref, et_ref, head_dim)
    v = _dot_nt(wv_ref[...], hn) if v_transposed else _dot(hn, wv_ref[...])
    lf = _log_sigmoid(_dot(hn, wf_ref[...]) + bf_ref[...])
    return k, v, lf


def _kv_sample_kernel(h_ref, g_ref, wk_ref, wv_ref, gk_ref, wf_ref, bf_ref, e_ref, et_ref,
                      k_ref, v_ref, lf_ref, kb_ref, vb_ref, *, head_dim):
    k, v, lf = _kv_core(h_ref, g_ref, wk_ref, wv_ref, gk_ref, wf_ref, bf_ref, e_ref, et_ref, head_dim, False)
    k_ref[...] = k
    v_ref[...] = v
    kb_ref[...] = k.astype(BF16)
    vb_ref[...] = v.astype(BF16)
    lf_ref[...] = lf[:, :N_HEADS]


def _kv_sample(h, g, wk, wv, gk_t, wf_p, bf_p, e, et):
    N, D = h.shape
    tm = TOK_BLOCK
    tok = pl.BlockSpec((tm, D), lambda i: (i, 0))
    return pl.pallas_call(
        functools.partial(_kv_sample_kernel, head_dim=D // N_HEADS),
        grid=(N // tm,),
        in_specs=[tok, _const_spec((1, D)), _const_spec((D, D)), _const_spec((D, D)), _const_spec((1, D)),
                  _const_spec((D, LANES)), _const_spec((1, LANES)), _const_spec((D, LANES)),
                  _const_spec((LANES, D))],
        out_specs=[tok, tok, pl.BlockSpec((tm, N_HEADS), lambda i: (i, 0)), tok, tok],
        out_shape=[jax.ShapeDtypeStruct((N, D), F32), jax.ShapeDtypeStruct((N, D), F32),
                   jax.ShapeDtypeStruct((N, N_HEADS), F32),
                   jax.ShapeDtypeStruct((N, D), BF16), jax.ShapeDtypeStruct((N, D), BF16)],
        compiler_params=_params("arbitrary"),
        name="shared_kv_sample",
    )(h, g, wk, wv, gk_t, wf_p, bf_p, e, et)


def _kv_prompt_kernel(h_ref, g_ref, wk_ref, wv_ref, gk_ref, wf_ref, bf_ref, e_ref, et_ref, tril_ref, spread_ref,
                      kt_ref, vt_ref, lft_ref, ct_ref, ka_ref, va_ref, carry_ref, *, nb, head_dim):
    i = pl.program_id(0)
    k, v_t, lf = _kv_core(h_ref, g_ref, wk_ref, wv_ref, gk_ref, wf_ref, bf_ref, e_ref, et_ref, head_dim, True)
    tm = k.shape[0]
    lane = lax.broadcasted_iota(jnp.int32, (tm, LANES), 1)
    lf = jnp.where(lane < N_HEADS, lf, 0.0)
    kt_ref[...] = k.T
    vt_ref[...] = v_t
    lft_ref[...] = lf.T[:N_HEADS]

    @pl.when(i % nb == 0)
    def _():
        carry_ref[...] = jnp.zeros_like(carry_ref)

    hi, mid, lo = (t.astype(F32) for t in _split3(lf))
    sums = _dot(tril_ref[...], (hi + pltpu.roll(mid, N_HEADS, 1) + pltpu.roll(lo, 2 * N_HEADS, 1)).astype(BF16))
    sums = sums + pltpu.roll(sums, LANES - N_HEADS, 1) + pltpu.roll(sums, LANES - 2 * N_HEADS, 1)
    c = jnp.where(lane < N_HEADS, sums, 0.0) + carry_ref[...]
    carry_ref[...] = c[tm - 1:, :]
    ct_ref[...] = c.T[:N_HEADS]

    hi, mid, lo = (t.astype(F32) for t in _split3(c * -LOG2E))
    terms = hi + pltpu.roll(mid, N_HEADS, 1) + pltpu.roll(lo, 2 * N_HEADS, 1)
    ones = jnp.where((lane >= head_dim + 3) & (lane < head_dim + 6), 1.0, 0.0)
    aug = _dot(terms.astype(BF16), spread_ref[...])
    low = lane < head_dim
    for p in range(k.shape[1] // LANES):
        kp = k[:, p * LANES:(p + 1) * LANES]
        even = jnp.where(low, kp, ones) + aug[:, (2 * p) * LANES:(2 * p + 1) * LANES]
        odd = jnp.where(low, pltpu.roll(kp, head_dim, 1), ones) + aug[:, (2 * p + 1) * LANES:(2 * p + 2) * LANES]
        ka_ref[:, (2 * p) * LANES:(2 * p + 1) * LANES] = even.astype(BF16)
        ka_ref[:, (2 * p + 1) * LANES:(2 * p + 2) * LANES] = odd.astype(BF16)

    va_ref[:, 0:head_dim, :] = v_t.reshape(N_HEADS, head_dim, tm).astype(BF16)
    sub = lax.broadcasted_iota(jnp.int32, (N_HEADS, V_EXTRA, tm), 1)
    va_ref[:, head_dim:, :] = jnp.where(sub == 0, 1.0, 0.0).astype(BF16)


def _kv_prompt(h, T, g, wk, wv_t, gk_t, wf_p, bf_p, e, et):
    N, D = h.shape
    B = N // T
    head_dim = D // N_HEADS
    tm = TOK_BLOCK
    nb = T // tm
    tok = pl.BlockSpec((tm, D), lambda i: (i, 0))
    tril = (jnp.arange(tm)[:, None] >= jnp.arange(tm)[None, :]).astype(BF16)
    src = jnp.arange(LANES)[:, None]
    dst = jnp.arange(N_HEADS * LANES)[None, :]
    spread = ((src < 3 * N_HEADS) & (dst // LANES == src % N_HEADS)
              & (dst % LANES == head_dim + src // N_HEADS)).astype(BF16)
    seq_t = lambda r: pl.BlockSpec((None, r, tm), lambda i: (i // nb, 0, i % nb))
    return pl.pallas_call(
        functools.partial(_kv_prompt_kernel, nb=nb, head_dim=head_dim),
        grid=(N // tm,),
        in_specs=[tok, _const_spec((1, D)), _const_spec((D, D)), _const_spec((D, D)), _const_spec((1, D)),
                  _const_spec((D, LANES)), _const_spec((1, LANES)), _const_spec((D, LANES)),
                  _const_spec((LANES, D)), _const_spec((tm, tm)), _const_spec((LANES, N_HEADS * LANES))],
        out_specs=[seq_t(D), seq_t(D), seq_t(N_HEADS), seq_t(N_HEADS),
                   pl.BlockSpec((tm, N_HEADS * LANES), lambda i: (i, 0)),
                   pl.BlockSpec((None, N_HEADS, head_dim + V_EXTRA, tm), lambda i: (i // nb, 0, 0, i % nb))],
        out_shape=[jax.ShapeDtypeStruct((B, D, T), F32), jax.ShapeDtypeStruct((B, D, T), F32),
                   jax.ShapeDtypeStruct((B, N_HEADS, T), F32), jax.ShapeDtypeStruct((B, N_HEADS, T), F32),
                   jax.ShapeDtypeStruct((N, N_HEADS * LANES), BF16),
                   jax.ShapeDtypeStruct((B, N_HEADS, head_dim + V_EXTRA, T), BF16)],
        scratch_shapes=[pltpu.VMEM((1, LANES), F32)],
        compiler_params=_params("arbitrary"),
        name="shared_kv_prompt",
    )(h, g, wk, wv_t, gk_t, wf_p, bf_p, e, et, tril, spread)


def _q_kernel(*refs, head_dim, transposed):
    if transposed:
        h_ref, g_ref, wq_ref, gq_ref, e_ref, et_ref, ct_ref, bound_ref, q_ref = refs
    else:
        h_ref, g_ref, wq_ref, gq_ref, e_ref, et_ref, q_ref = refs
    xn = _rms(h_ref[...], g_ref[...]).astype(BF16)
    q = _proj_head_rms(xn, wq_ref, gq_ref, e_ref, et_ref, head_dim)
    if not transposed:
        q_ref[...] = (q * (head_dim ** -0.5)).astype(BF16)
        return
    tm = q.shape[0]
    q_t = (q * (head_dim ** -0.5 * LOG2E)).T.reshape(N_HEADS, head_dim, tm)
    q_ref[:, 0:head_dim, :] = q_t.astype(BF16)
    shape = (N_HEADS, LANES - head_dim, tm)
    sub = lax.broadcasted_iota(jnp.int32, shape, 1)
    hi, mid, lo = (jnp.broadcast_to(t.astype(F32)[:, None, :], shape)
                   for t in _split3(ct_ref[...] * LOG2E - bound_ref[...]))
    tail = jnp.where(sub < 3, 1.0, jnp.where(sub == 3, hi, jnp.where(sub == 4, mid, jnp.where(sub == 5, lo, 0.0))))
    q_ref[:, head_dim:, :] = tail.astype(BF16)


def _qproj(h, g, wq, layer, gq_t, e, et, ct=None, bound=None):
    N, D = h.shape
    tm = TOK_BLOCK
    tok = pl.BlockSpec((tm, D), lambda i: (i, 0))
    in_specs = [tok, _const_spec((1, D)), _layer_spec((D, D), layer), _const_spec((1, D)),
                _const_spec((D, LANES)), _const_spec((LANES, D))]
    args = [h, g, wq, gq_t, e, et]
    if ct is None:
        out_spec, out_shape = tok, jax.ShapeDtypeStruct((N, D), BF16)
    else:
        T = ct.shape[2]
        nb = T // tm
        in_specs += [pl.BlockSpec((None, N_HEADS, tm), lambda i: (i // nb, 0, i % nb)), _const_spec((1, 1))]
        args += [ct, bound]
        out_spec = pl.BlockSpec((None, N_HEADS, LANES, tm), lambda i: (i // nb, 0, 0, i % nb))
        out_shape = jax.ShapeDtypeStruct((N // T, N_HEADS, LANES, T), BF16)
    return pl.pallas_call(
        functools.partial(_q_kernel, head_dim=D // N_HEADS, transposed=ct is not None),
        grid=(N // tm,),
        in_specs=in_specs,
        out_specs=out_spec,
        out_shape=out_shape,
        compiler_params=_params("arbitrary"),
        name="q_proj" if ct is None else "q_proj_t",
    )(*args)


def _attn_prompt_kernel(q_ref, k_ref, v_ref, o_ref, *, blk, head_dim, bounded):
    qi = pl.program_id(2)
    n_h = q_ref.shape[0]
    q_t = [q_ref[x] for x in range(n_h)]
    key_iota = lax.broadcasted_iota(jnp.int32, (blk, blk), 0)
    qry_iota = lax.broadcasted_iota(jnp.int32, (blk, blk), 1)

    def step(j0, n_blocks, carry, diagonal_last):
        offs = [pl.multiple_of((j0 + b) * blk, blk) for b in range(n_blocks)]
        s_all = [[_dot(k_ref[pl.ds(off, blk), x * LANES:(x + 1) * LANES], q_t[x]) for x in range(n_h)]
                 for off in offs]
        out = []
        for x, (m, acc) in enumerate(carry):
            for b, off in enumerate(offs):
                s = s_all[b][x]
                if diagonal_last and b == n_blocks - 1:
                    s = jnp.where(key_iota <= qry_iota, s, NEG_BIG)
                if bounded:
                    acc = acc + _dot(v_ref[x, :, pl.ds(off, blk)], jnp.exp2(s).astype(BF16))
                    continue
                m_new = jnp.maximum(m, jnp.max(s, axis=0, keepdims=True))
                p = jnp.exp2(s - m_new).astype(BF16)
                acc = jnp.exp2(m - m_new) * acc + _dot(v_ref[x, :, pl.ds(off, blk)], p)
                m = m_new
            out.append((m, acc))
        return tuple(out)

    init = tuple((jnp.full((1, blk), NEG_BIG, F32), jnp.zeros((v_ref.shape[1], blk), F32)) for _ in range(n_h))
    n_wide = qi // ATT_UNROLL
    carry = lax.fori_loop(0, n_wide, lambda jj, c: step(jj * ATT_UNROLL, ATT_UNROLL, c, False), init)
    j0 = n_wide * ATT_UNROLL
    tails = [functools.partial(step, j0, r + 1, diagonal_last=True) for r in range(ATT_UNROLL)]
    for x, (_, acc) in enumerate(lax.switch(qi - j0, tails, carry)):
        o_ref[x] = (acc[:head_dim] / acc[head_dim:head_dim + 1]).astype(o_ref.dtype)


def _attn_prompt(q_t, k_aug, v_aug, bounded):
    B, H, _, T = q_t.shape
    vr = v_aug.shape[2]
    head_dim = vr - V_EXTRA
    blk = ATT_BLOCK
    hb = ATT_HEADS
    return pl.pallas_call(
        functools.partial(_attn_prompt_kernel, blk=blk, head_dim=head_dim, bounded=bounded),
        grid=(B, H // hb, T // blk),
        in_specs=[
            pl.BlockSpec((None, hb, LANES, blk), lambda b, h, i: (b, h, 0, i)),
            pl.BlockSpec((None, T, hb * LANES), lambda b, h, i: (b, 0, h)),
            pl.BlockSpec((None, hb, vr, T), lambda b, h, i: (b, h, 0, 0)),
        ],
        out_specs=pl.BlockSpec((None, hb, head_dim, blk), lambda b, h, i: (b, h, 0, i)),
        out_shape=jax.ShapeDtypeStruct((B, H, head_dim, T), BF16),
        compiler_params=_params("arbitrary", "arbitrary", "arbitrary"),
        name="attn_prompt_bounded" if bounded else "attn_prompt",
    )(q_t, k_aug, v_aug)


def _attn_sample_kernel(pt_ref, q_ref, lfn_ref, kn_ref, vn_ref, *refs, n_pages, page, ts, head_dim, new_pad):
    del pt_ref
    lf_refs = refs[:n_pages]
    k_refs = refs[n_pages:2 * n_pages]
    v_refs = refs[2 * n_pages:3 * n_pages]
    o_ref = refs[3 * n_pages]
    D = q_ref.shape[1]
    n_pairs = D // LANES
    tp = 8
    lane = lax.broadcasted_iota(jnp.int32, (tp, LANES), 1)
    first = lane < head_dim
    tri = (lax.broadcasted_iota(jnp.int32, (page, page), 0)
           <= lax.broadcasted_iota(jnp.int32, (page, page), 1)).astype(BF16)

    q = q_ref[...].astype(F32)
    q8 = jnp.concatenate([q, jnp.zeros((tp - ts, D), F32)], axis=0)
    q_pairs = []
    for p in range(n_pairs):
        qp = q8[:, p * LANES:(p + 1) * LANES]
        q_pairs.append(jnp.concatenate([jnp.where(first, qp, 0.0), jnp.where(first, 0.0, qp)], axis=0).astype(BF16))

    def running_sum(lf_t, carry):
        parts = _dot(jnp.concatenate(_split3(lf_t), axis=0), tri)
        return parts[:N_HEADS] + parts[N_HEADS:2 * N_HEADS] + parts[2 * N_HEADS:] + carry

    def head_rows(c):
        n = c.shape[1]
        return jnp.broadcast_to(c[:, None, :], (N_HEADS, tp, n)).reshape(N_HEADS * tp, n)

    carry = jnp.zeros((N_HEADS, 1), F32)
    s_pages = []
    for j in range(n_pages):
        c = running_sum(lf_refs[j][...], carry)
        carry = c[:, page - 1:]
        s = [_dot(q_pairs[p], k_refs[j][p * LANES:(p + 1) * LANES, :].astype(BF16)) for p in range(n_pairs)]
        s_pages.append(jnp.concatenate(s, axis=0) - head_rows(c))
    c = running_sum(lfn_ref[...], carry)[:, :new_pad]
    s = [_dot_nt(q_pairs[p], kn_ref[:, p * LANES:(p + 1) * LANES]) for p in range(n_pairs)]
    t_row = lax.broadcasted_iota(jnp.int32, (N_HEADS * tp, new_pad), 0) % tp
    t_key = lax.broadcasted_iota(jnp.int32, (N_HEADS * tp, new_pad), 1)
    s_new = jnp.where(t_key <= t_row, jnp.concatenate(s, axis=0) - head_rows(c), NEG_BIG)

    m_el = s_pages[0]
    for s in s_pages[1:]:
        m_el = jnp.maximum(m_el, s)
    m = jnp.maximum(jnp.max(m_el, axis=1, keepdims=True), jnp.max(s_new, axis=1, keepdims=True))
    l_el = jnp.zeros((N_HEADS * tp, page), F32)
    acc_t = jnp.zeros((D, N_HEADS * tp), F32)
    for j in range(n_pages):
        pj = jnp.exp(s_pages[j] - m)
        l_el = l_el + pj
        acc_t = acc_t + _dot(v_refs[j][...].astype(BF16), pj.T.astype(BF16))
    p_new = jnp.exp(s_new - m)
    l = jnp.sum(l_el, axis=1, keepdims=True) + jnp.sum(p_new, axis=1, keepdims=True)
    o = (acc_t.T + _dot(p_new.astype(BF16), vn_ref[...])) / l
    row_head = lax.broadcasted_iota(jnp.int32, (N_HEADS * tp, D), 0) // tp
    col_head = lax.broadcasted_iota(jnp.int32, (N_HEADS * tp, D), 1) // head_dim
    own = jnp.where(row_head == col_head, o, 0.0).reshape(N_HEADS, tp, D)
    o_ref[...] = jnp.sum(own, axis=0)[:ts].astype(o_ref.dtype)


def _attn_sample(q, lf_new_t, k_new, v_new, cache_k, cache_v, cache_logf, page_table):
    bs, ts, D = q.shape
    n_pages = page_table.shape[1]
    n_pool, page = cache_k.shape[0], cache_k.shape[1]
    assert page == LANES
    new_pad = k_new.shape[1]
    ck = jnp.transpose(cache_k, (0, 2, 3, 1)).reshape(n_pool, D, page)
    cv = jnp.transpose(cache_v, (0, 2, 3, 1)).reshape(n_pool, D, page)
    cl = jnp.transpose(cache_logf, (0, 2, 1))
    seq = lambda n: pl.BlockSpec((None, n, D), lambda b, pt: (b, 0, 0))
    paged = lambda r: [pl.BlockSpec((None, r, page), lambda b, pt, j=j: (pt[b, j], 0, 0)) for j in range(n_pages)]
    return pl.pallas_call(
        functools.partial(_attn_sample_kernel, n_pages=n_pages, page=page, ts=ts,
                          head_dim=D // N_HEADS, new_pad=new_pad),
        grid_spec=pltpu.PrefetchScalarGridSpec(
            num_scalar_prefetch=1,
            grid=(bs,),
            in_specs=[seq(ts), pl.BlockSpec((None, N_HEADS, page), lambda b, pt: (b, 0, 0)),
                      seq(new_pad), seq(new_pad)] + paged(N_HEADS) + paged(D) + paged(D),
            out_specs=seq(ts),
        ),
        out_shape=jax.ShapeDtypeStruct((bs, ts, D), BF16),
        compiler_params=_params("arbitrary"),
        name="attn_sample",
    )(page_table, q, lf_new_t, k_new, v_new, *([cl] * n_pages), *([ck] * n_pages), *([cv] * n_pages))


def kernel(x_prompt, x_sample, state_pool, cache_k, cache_v, cache_logf, page_table, g_pool, w_pool, pool_scale,
           g_attn, w_q, g_q, w_o, g_kv, w_k, w_v, g_k, w_f, b_f, g_mlp, w_up, w_down):
    Bp, Tp, D = x_prompt.shape
    Bs, Ts, _ = x_sample.shape
    n_a = g_pool.shape[0]
    n_b = g_attn.shape[0]
    head_dim = D // N_HEADS
    past = page_table.shape[1] * cache_k.shape[1]
    new_pad = 16

    row = lambda a: a.reshape(1, -1).astype(F32)
    w_pool_b, w_up_b, w_down_b, w_q_b, w_o_b = (w.astype(BF16) for w in (w_pool, w_up, w_down, w_q, w_o))
    w_k_b, w_v_b = w_k.astype(BF16), w_v.astype(BF16)
    w_f_p = jnp.pad(w_f, ((0, 0), (0, LANES - N_HEADS))).astype(BF16)
    b_f_p = jnp.pad(b_f.astype(F32), (0, LANES - N_HEADS)).reshape(1, LANES)
    e = (jnp.arange(D)[:, None] // head_dim == jnp.arange(LANES)[None, :]).astype(BF16)
    et = e.T
    g_k_t = row(jnp.tile(g_k, N_HEADS))

    h_p = x_prompt
    h_s = jnp.transpose(x_sample, (1, 0, 2))
    st_p, st_s = [], []
    for l in range(n_a):
        args = (row(g_pool[l]), w_pool_b, l, row(pool_scale[l]))
        h_p, tail = _pool_prompt(h_p, *args)
        st_p.append(tail[:, HALO - POOL_BUF:])
        h_p = _mlp(h_p.reshape(Bp * Tp, D), row(g_mlp[l]), w_up_b, w_down_b, l).reshape(Bp, Tp, D)
        h_s, st = _pool_sample(h_s, jnp.transpose(state_pool[l], (1, 0, 2)), *args, past)
        st_s.append(jnp.transpose(st, (1, 0, 2)))
        h_s = _mlp(h_s.reshape(Ts * Bs, D), row(g_mlp[l]), w_up_b, w_down_b, l).reshape(Ts, Bs, D)
    pool_state_prompt = jnp.stack(st_p, axis=0)
    pool_state_sample = jnp.stack(st_s, axis=0)

    h_p = h_p.reshape(Bp * Tp, D)
    h_s = jnp.transpose(h_s, (1, 0, 2)).reshape(Bs * Ts, D)

    kv_tail = (g_k_t, w_f_p, b_f_p, e, et)
    kt_p, vt_p, lft_p, ct_p, ka_p, va_p = _kv_prompt(h_p, Tp, row(g_kv), w_k_b, w_v_b.T, *kv_tail)
    k_s, v_s, lf_s, kb_s, vb_s = _kv_sample(h_s, row(g_kv), w_k_b, w_v_b, *kv_tail)
    pad_new = lambda a, n: jnp.pad(a.reshape(Bs, Ts, -1), ((0, 0), (0, n - Ts), (0, 0)))
    lfn_t = jnp.transpose(pad_new(lf_s, cache_logf.shape[1]), (0, 2, 1))
    kn, vn = pad_new(kb_s, new_pad), pad_new(vb_s, new_pad)
    ka_p = ka_p.reshape(Bp, Tp, N_HEADS * LANES)

    for l in range(n_b):
        j = n_a + l
        g_q_t = row(jnp.tile(g_q[l], N_HEADS))
        bound = (1.02 * LOG2E * head_dim ** 0.5) * jnp.max(jnp.abs(g_q[l])) * jnp.max(jnp.abs(g_k))
        q_p = _qproj(h_p, row(g_attn[l]), w_q_b, l, g_q_t, e, et, ct_p, bound.reshape(1, 1).astype(F32))
        o_p = lax.cond(2.0 * bound < MAX_SELF_DEFICIT,
                       functools.partial(_attn_prompt, bounded=True),
                       functools.partial(_attn_prompt, bounded=False), q_p, ka_p, va_p).reshape(Bp, D, Tp)
        h_p = _mlp(h_p, row(g_mlp[j]), w_up_b, w_down_b, j, o_p, w_o_b, l, o_transposed=True)
        q_s = _qproj(h_s, row(g_attn[l]), w_q_b, l, g_q_t, e, et)
        o_s = _attn_sample(q_s.reshape(Bs, Ts, D), lfn_t, kn, vn, cache_k, cache_v, cache_logf, page_table)
        h_s = _mlp(h_s, row(g_mlp[j]), w_up_b, w_down_b, j, o_s.reshape(Bs * Ts, D), w_o_b, l)

    hd_t = lambda a: jnp.transpose(a.reshape(Bp, N_HEADS, head_dim, Tp), (0, 3, 1, 2))
    hd4 = lambda a: a.reshape(Bs, Ts, N_HEADS, head_dim)
    return (h_p.reshape(Bp, Tp, D), h_s.reshape(Bs, Ts, D), pool_state_prompt, pool_state_sample,
            hd_t(kt_p), hd_t(vt_p), jnp.transpose(lft_p, (0, 2, 1)).astype(cache_logf.dtype),
            hd4(k_s), hd4(v_s), lf_s.reshape(Bs, Ts, N_HEADS).astype(cache_logf.dtype))
```

```python
import functools

import jax
import jax.numpy as jnp
from jax import lax
from jax.experimental import pallas as pl
from jax.experimental.pallas import tpu as pltpu

F32 = jnp.float32
BF16 = jnp.bfloat16

EPS = 1e-6
POOL_WINDOWS = (2, 4, 8, 16)
POOL_BUF = max(POOL_WINDOWS) - 1
assert list(POOL_WINDOWS) == sorted(POOL_WINDOWS) and all(w & (w - 1) == 0 for w in POOL_WINDOWS)
N_HEADS = 16
LANES = 128
HALO = 16
VMEM_LIMIT = 56 * 1024 * 1024
NEG_BIG = -1e30

TOK_BLOCK = 512
FF_CHUNK = 1024
PROJ_SPLITS = 4
ATT_BLOCK = 512
ATT_UNROLL = 4
ATT_HEADS = 4
V_EXTRA = 16
LOG2E = 1.4426950408889634
MAX_SELF_DEFICIT = 100.0


def _dot(a, b):
    return jnp.dot(a, b, preferred_element_type=F32)


def _dot_nt(a, b):
    return lax.dot_general(a, b, (((1,), (1,)), ((), ())), preferred_element_type=F32)


def _rms(x, g):
    ms = jnp.mean(x * x, axis=-1, keepdims=True)
    return x * lax.rsqrt(ms + EPS) * g


def _split2(x):
    hi = x.astype(BF16)
    lo = (x - hi.astype(F32)).astype(BF16)
    return hi, lo


def _split3(x):
    hi = x.astype(BF16)
    r1 = x - hi.astype(F32)
    mid = r1.astype(BF16)
    lo = (r1 - mid.astype(F32)).astype(BF16)
    return hi, mid, lo


def _head_rms(x, g_tiled, e, et, head_dim):
    hi, lo = _split2(x * x)
    ss = _dot(hi, e) + _dot(lo, e)
    r = lax.rsqrt(ss * (1.0 / head_dim) + EPS)
    rh, rl = _split2(r)
    return x * (_dot(rh, et) + _dot(rl, et)) * g_tiled


def _proj_head_rms(x, w_ref, g_ref, e_ref, et_ref, head_dim):
    width = w_ref.shape[1] // PROJ_SPLITS
    cols = [slice(i * width, (i + 1) * width) for i in range(PROJ_SPLITS)]
    ys = [_dot(x, w_ref[:, c]) for c in cols]
    return jnp.concatenate([_head_rms(y, g_ref[:, c], e_ref[c, :], et_ref[:, c], head_dim)
                            for y, c in zip(ys, cols)], axis=1)


def _const_spec(shape):
    return pl.BlockSpec(shape, lambda *_: (0,) * len(shape), pipeline_mode=pl.Buffered(1))


def _layer_spec(shape, layer):
    return pl.BlockSpec((None,) + tuple(shape), lambda *_: (layer,) + (0,) * len(shape),
                        pipeline_mode=pl.Buffered(1))


def _params(*sem):
    return pltpu.CompilerParams(dimension_semantics=sem, vmem_limit_bytes=VMEM_LIMIT)


def _pool_prompt_kernel(x_ref, halo_ref, g_ref, w_ref, sc_ref, o_ref, st_ref, *, tb, gdim):
    i = pl.program_id(1)
    g = g_ref[...]
    x = x_ref[...]
    xn = _rms(x, g)
    hn = _rms(halo_ref[...], g)
    sums = jnp.concatenate([jnp.where(i > 0, hn, 0.0), xn], axis=0)
    width, windows = 1, []
    for w in POOL_WINDOWS:
        while width < w:
            sums = sums + pltpu.roll(sums, width, 0)
            width *= 2
        windows.append(sums[HALO:, :gdim])
        sums = sums[:, gdim:]
    pos = i * tb + lax.broadcasted_iota(jnp.int32, (tb, 1), 0)
    sc = sc_ref[...]
    for gi, w in enumerate(POOL_WINDOWS):
        sl = slice(gi * gdim, (gi + 1) * gdim)
        cnt = jnp.minimum(pos + 1, w).astype(F32)
        pooled = windows[gi] / cnt - xn[:, sl]
        mixed = _dot(pooled.astype(BF16), w_ref[gi]) * sc[:, sl]
        o_ref[:, sl] = x[:, sl] + mixed
    st_ref[...] = xn[tb - HALO:, :]


def _pool_prompt(h, g, w, layer, sc):
    B, T, D = h.shape
    tb = TOK_BLOCK
    gdim = D // len(POOL_WINDOWS)
    r = tb // HALO
    return pl.pallas_call(
        functools.partial(_pool_prompt_kernel, tb=tb, gdim=gdim),
        grid=(B, T // tb),
        in_specs=[
            pl.BlockSpec((None, tb, D), lambda b, i: (b, i, 0)),
            pl.BlockSpec((None, HALO, D), lambda b, i: (b, jnp.maximum(i * r - 1, 0), 0)),
            _const_spec((1, D)),
            _layer_spec((len(POOL_WINDOWS), gdim, gdim), layer),
            _const_spec((1, D)),
        ],
        out_specs=[
            pl.BlockSpec((None, tb, D), lambda b, i: (b, i, 0)),
            pl.BlockSpec((None, HALO, D), lambda b, i: (b, 0, 0)),
        ],
        out_shape=[jax.ShapeDtypeStruct((B, T, D), F32), jax.ShapeDtypeStruct((B, HALO, D), F32)],
        compiler_params=_params("arbitrary", "arbitrary"),
        name="pool_prompt",
    )(h, h, g, w, sc)


def _pool_sample_kernel(x_ref, pf_ref, g_ref, w_ref, sc_ref, o_ref, st_ref, *, ts, bb, gdim, cnts):
    g = g_ref[...]
    sc = sc_ref[...]
    xs = [x_ref[t] for t in range(ts)]
    xn = [_rms(x, g) for x in xs]
    ext = [pf_ref[r] for r in range(POOL_BUF)] + xn
    for gi, w in enumerate(POOL_WINDOWS):
        sl = slice(gi * gdim, (gi + 1) * gdim)
        pooled = []
        for t in range(ts):
            acc = ext[POOL_BUF + t][:, sl]
            for j in range(1, w):
                acc = acc + ext[POOL_BUF + t - j][:, sl]
            pooled.append(acc / cnts[t][gi] - xn[t][:, sl])
        mixed = _dot(jnp.concatenate(pooled, axis=0).astype(BF16), w_ref[gi]) * sc[:, sl]
        for t in range(ts):
            o_ref[t, :, sl] = xs[t][:, sl] + mixed[t * bb:(t + 1) * bb]
    for r in range(POOL_BUF):
        st_ref[r] = ext[ts + r]


def _pool_sample(h_tm, prefix_tm, g, w, layer, sc, past):
    ts, bs, D = h_tm.shape
    bb = 32
    gdim = D // len(POOL_WINDOWS)
    cnts = tuple(tuple(float(min(past + t + 1, w)) for w in POOL_WINDOWS) for t in range(ts))
    return pl.pallas_call(
        functools.partial(_pool_sample_kernel, ts=ts, bb=bb, gdim=gdim, cnts=cnts),
        grid=(bs // bb,),
        in_specs=[
            pl.BlockSpec((ts, bb, D), lambda i: (0, i, 0)),
            pl.BlockSpec((POOL_BUF, bb, D), lambda i: (0, i, 0)),
            _const_spec((1, D)),
            _layer_spec((len(POOL_WINDOWS), gdim, gdim), layer),
            _const_spec((1, D)),
        ],
        out_specs=[
            pl.BlockSpec((ts, bb, D), lambda i: (0, i, 0)),
            pl.BlockSpec((POOL_BUF, bb, D), lambda i: (0, i, 0)),
        ],
        out_shape=[jax.ShapeDtypeStruct((ts, bs, D), F32), jax.ShapeDtypeStruct((POOL_BUF, bs, D), F32)],
        compiler_params=_params("arbitrary"),
        name="pool_sample",
    )(h_tm, prefix_tm, g, w, sc)


def _mlp_kernel(*refs, o_mode, n_chunks):
    if o_mode is None:
        h_ref, g_ref, wu_ref, wd_ref, out_ref = refs
        h = h_ref[...]
    else:
        h_ref, o_ref, wo_ref, g_ref, wu_ref, wd_ref, out_ref = refs
        o = o_ref[...]
        if o_mode == "transposed":
            proj = lax.dot_general(o, wo_ref[...], (((0,), (0,)), ((), ())), preferred_element_type=F32)
        else:
            proj = _dot(o, wo_ref[...])
        h = h_ref[...] + proj
    xn = _rms(h, g_ref[...]).astype(BF16)
    acc = h
    for c in range(n_chunks):
        cs = slice(c * FF_CHUNK, (c + 1) * FF_CHUNK)
        a = jnp.maximum(_dot(xn, wu_ref[:, cs]), 0.0)
        acc = acc + _dot((a * a).astype(BF16), wd_ref[cs, :])
    out_ref[...] = acc


def _mlp(h, g, w_up, w_down, layer, o=None, w_o=None, o_layer=None, o_transposed=False):
    N, D = h.shape
    F = w_up.shape[2]
    tm = TOK_BLOCK
    tok = pl.BlockSpec((tm, D), lambda i: (i, 0))
    in_specs = [tok]
    args = [h]
    o_mode = None
    if o is not None:
        if o_transposed:
            o_mode = "transposed"
            nb = o.shape[2] // tm
            in_specs.append(pl.BlockSpec((None, D, tm), lambda i: (i // nb, 0, i % nb)))
        else:
            o_mode = "rows"
            in_specs.append(tok)
        in_specs.append(_layer_spec((D, D), o_layer))
        args += [o, w_o]
    in_specs += [_const_spec((1, D)), _layer_spec((D, F), layer), _layer_spec((F, D), layer)]
    args += [g, w_up, w_down]
    return pl.pallas_call(
        functools.partial(_mlp_kernel, o_mode=o_mode, n_chunks=F // FF_CHUNK),
        grid=(N // tm,),
        in_specs=in_specs,
        out_specs=tok,
        out_shape=jax.ShapeDtypeStruct((N, D), F32),
        compiler_params=_params("arbitrary"),
        name="mlp" if o is None else "mlp_o",
    )(*args)


def _log_sigmoid(z):
    return -(jnp.maximum(-z, 0.0) + jnp.log1p(jnp.exp(-jnp.abs(z))))


def _gates(hn, wf_ref, bf_ref):
    return _log_sigmoid(_dot(hn, wf_ref[...]) + bf_ref[...])


def _kv_sample_kernel(h_ref, g_ref, wk_ref, wv_ref, gk_ref, wf_ref, bf_ref, e_ref, et_ref,
                      k_ref, v_ref, lf_ref, kb_ref, vb_ref, *, head_dim):
    hn = _rms(h_ref[...], g_ref[...]).astype(BF16)
    k = _proj_head_rms(hn, wk_ref, gk_ref, e_ref, et_ref, head_dim)
    v = _dot(hn, wv_ref[...])
    lf = _gates(hn, wf_ref, bf_ref)
    k_ref[...] = k
    v_ref[...] = v
    kb_ref[...] = k.astype(BF16)
    vb_ref[...] = v.astype(BF16)
    lf_ref[...] = lf[:, :N_HEADS]


def _kv_sample(h, g, wk, wv, gk_t, wf_p, bf_p, e, et):
    N, D = h.shape
    tm = TOK_BLOCK
    tok = pl.BlockSpec((tm, D), lambda i: (i, 0))
    return pl.pallas_call(
        functools.partial(_kv_sample_kernel, head_dim=D // N_HEADS),
        grid=(N // tm,),
        in_specs=[tok, _const_spec((1, D)), _const_spec((D, D)), _const_spec((D, D)), _const_spec((1, D)),
                  _const_spec((D, LANES)), _const_spec((1, LANES)), _const_spec((D, LANES)),
                  _const_spec((LANES, D))],
        out_specs=[tok, tok, pl.BlockSpec((tm, N_HEADS), lambda i: (i, 0)), tok, tok],
        out_shape=[jax.ShapeDtypeStruct((N, D), F32), jax.ShapeDtypeStruct((N, D), F32),
                   jax.ShapeDtypeStruct((N, N_HEADS), F32),
                   jax.ShapeDtypeStruct((N, D), BF16), jax.ShapeDtypeStruct((N, D), BF16)],
        compiler_params=_params("arbitrary"),
        name="shared_kv_sample",
    )(h, g, wk, wv, gk_t, wf_p, bf_p, e, et)


def _kv_prompt_kernel(h_ref, g_ref, wkt_ref, wvt_ref, gk_ref, wf_ref, bf_ref, tril_ref, spread_ref,
                      kt_ref, vt_ref, lft_ref, ct_ref, ka_ref, va_ref, carry_ref, *, nb, head_dim):
    i = pl.program_id(0)
    hn = _rms(h_ref[...], g_ref[...]).astype(BF16)
    tm = hn.shape[0]
    y = _dot_nt(wkt_ref[...], hn).reshape(N_HEADS, head_dim, tm)
    v_t = _dot_nt(wvt_ref[...], hn)
    lf = _gates(hn, wf_ref, bf_ref)
    k_t = (y * lax.rsqrt(jnp.mean(y * y, axis=1, keepdims=True) + EPS) * gk_ref[...]).reshape(N_HEADS * head_dim, tm)
    k = k_t.T
    lane = lax.broadcasted_iota(jnp.int32, (tm, LANES), 1)
    lf = jnp.where(lane < N_HEADS, lf, 0.0)
    kt_ref[...] = k_t
    vt_ref[...] = v_t
    lft_ref[...] = lf.T[:N_HEADS]

    @pl.when(i % nb == 0)
    def _():
        carry_ref[...] = jnp.zeros_like(carry_ref)

    hi, mid, lo = (t.astype(F32) for t in _split3(lf))
    sums = _dot(tril_ref[...], (hi + pltpu.roll(mid, N_HEADS, 1) + pltpu.roll(lo, 2 * N_HEADS, 1)).astype(BF16))
    sums = sums + pltpu.roll(sums, LANES - N_HEADS, 1) + pltpu.roll(sums, LANES - 2 * N_HEADS, 1)
    c = jnp.where(lane < N_HEADS, sums, 0.0) + carry_ref[...]
    carry_ref[...] = c[tm - 1:, :]
    ct_ref[...] = c.T[:N_HEADS]

    hi, mid, lo = (t.astype(F32) for t in _split3(c * -LOG2E))
    terms = hi + pltpu.roll(mid, N_HEADS, 1) + pltpu.roll(lo, 2 * N_HEADS, 1)
    ones = jnp.where((lane >= head_dim + 3) & (lane < head_dim + 6), 1.0, 0.0)
    aug = _dot(terms.astype(BF16), spread_ref[...])
    low = lane < head_dim
    for p in range(k.shape[1] // LANES):
        kp = k[:, p * LANES:(p + 1) * LANES]
        even = jnp.where(low, kp, ones) + aug[:, (2 * p) * LANES:(2 * p + 1) * LANES]
        odd = jnp.where(low, pltpu.roll(kp, head_dim, 1), ones) + aug[:, (2 * p + 1) * LANES:(2 * p + 2) * LANES]
        ka_ref[:, (2 * p) * LANES:(2 * p + 1) * LANES] = even.astype(BF16)
        ka_ref[:, (2 * p + 1) * LANES:(2 * p + 2) * LANES] = odd.astype(BF16)

    va_ref[:, 0:head_dim, :] = v_t.reshape(N_HEADS, head_dim, tm).astype(BF16)
    sub = lax.broadcasted_iota(jnp.int32, (N_HEADS, V_EXTRA, tm), 1)
    va_ref[:, head_dim:, :] = jnp.where(sub == 0, 1.0, 0.0).astype(BF16)


def _kv_prompt(h, T, g, wk_t, wv_t, gk_col, wf_p, bf_p):
    N, D = h.shape
    B = N // T
    head_dim = D // N_HEADS
    tm = TOK_BLOCK
    nb = T // tm
    tok = pl.BlockSpec((tm, D), lambda i: (i, 0))
    tril = (jnp.arange(tm)[:, None] >= jnp.arange(tm)[None, :]).astype(BF16)
    src = jnp.arange(LANES)[:, None]
    dst = jnp.arange(N_HEADS * LANES)[None, :]
    spread = ((src < 3 * N_HEADS) & (dst // LANES == src % N_HEADS)
              & (dst % LANES == head_dim + src // N_HEADS)).astype(BF16)
    seq_t = lambda r: pl.BlockSpec((None, r, tm), lambda i: (i // nb, 0, i % nb))
    return pl.pallas_call(
        functools.partial(_kv_prompt_kernel, nb=nb, head_dim=head_dim),
        grid=(N // tm,),
        in_specs=[tok, _const_spec((1, D)), _const_spec((D, D)), _const_spec((D, D)),
                  _const_spec((N_HEADS, head_dim, 1)), _const_spec((D, LANES)), _const_spec((1, LANES)),
                  _const_spec((tm, tm)), _const_spec((LANES, N_HEADS * LANES))],
        out_specs=[seq_t(D), seq_t(D), seq_t(N_HEADS), seq_t(N_HEADS),
                   pl.BlockSpec((tm, N_HEADS * LANES), lambda i: (i, 0)),
                   pl.BlockSpec((None, N_HEADS, head_dim + V_EXTRA, tm), lambda i: (i // nb, 0, 0, i % nb))],
        out_shape=[jax.ShapeDtypeStruct((B, D, T), F32), jax.ShapeDtypeStruct((B, D, T), F32),
                   jax.ShapeDtypeStruct((B, N_HEADS, T), F32), jax.ShapeDtypeStruct((B, N_HEADS, T), F32),
                   jax.ShapeDtypeStruct((N, N_HEADS * LANES), BF16),
                   jax.ShapeDtypeStruct((B, N_HEADS, head_dim + V_EXTRA, T), BF16)],
        scratch_shapes=[pltpu.VMEM((1, LANES), F32)],
        compiler_params=_params("arbitrary"),
        name="shared_kv_prompt",
    )(h, g, wk_t, wv_t, gk_col, wf_p, bf_p, tril, spread)


def _q_kernel(h_ref, g_ref, wq_ref, gq_ref, e_ref, et_ref, q_ref, *, head_dim):
    xn = _rms(h_ref[...], g_ref[...]).astype(BF16)
    q = _proj_head_rms(xn, wq_ref, gq_ref, e_ref, et_ref, head_dim)
    q_ref[...] = (q * (head_dim ** -0.5)).astype(BF16)


def _qproj(h, g, wq, layer, gq_t, e, et):
    N, D = h.shape
    tm = TOK_BLOCK
    tok = pl.BlockSpec((tm, D), lambda i: (i, 0))
    return pl.pallas_call(
        functools.partial(_q_kernel, head_dim=D // N_HEADS),
        grid=(N // tm,),
        in_specs=[tok, _const_spec((1, D)), _layer_spec((D, D), layer), _const_spec((1, D)),
                  _const_spec((D, LANES)), _const_spec((LANES, D))],
        out_specs=tok,
        out_shape=jax.ShapeDtypeStruct((N, D), BF16),
        compiler_params=_params("arbitrary"),
        name="q_proj",
    )(h, g, wq, gq_t, e, et)


def _q_t_kernel(h_ref, g_ref, wqt_ref, gq_ref, ct_ref, bound_ref, q_ref, *, head_dim):
    xn = _rms(h_ref[...], g_ref[...]).astype(BF16)
    tm = xn.shape[0]
    y = _dot_nt(wqt_ref[...], xn).reshape(N_HEADS, head_dim, tm)
    r = lax.rsqrt(jnp.mean(y * y, axis=1, keepdims=True) + EPS)
    q_ref[:, 0:head_dim, :] = (y * r * gq_ref[...]).astype(BF16)
    shape = (N_HEADS, LANES - head_dim, tm)
    sub = lax.broadcasted_iota(jnp.int32, shape, 1)
    hi, mid, lo = (jnp.broadcast_to(t.astype(F32)[:, None, :], shape)
                   for t in _split3(ct_ref[...] * LOG2E - bound_ref[...]))
    tail = jnp.where(sub < 3, 1.0, jnp.where(sub == 3, hi, jnp.where(sub == 4, mid, jnp.where(sub == 5, lo, 0.0))))
    q_ref[:, head_dim:, :] = tail.astype(BF16)


def _qproj_t(h, g, wq_t, layer, gq_col, ct, bound):
    N, D = h.shape
    tm = TOK_BLOCK
    T = ct.shape[2]
    nb = T // tm
    head_dim = D // N_HEADS
    return pl.pallas_call(
        functools.partial(_q_t_kernel, head_dim=head_dim),
        grid=(N // tm,),
        in_specs=[pl.BlockSpec((tm, D), lambda i: (i, 0)), _const_spec((1, D)), _layer_spec((D, D), layer),
                  _const_spec((N_HEADS, head_dim, 1)),
                  pl.BlockSpec((None, N_HEADS, tm), lambda i: (i // nb, 0, i % nb)), _const_spec((1, 1))],
        out_specs=pl.BlockSpec((None, N_HEADS, LANES, tm), lambda i: (i // nb, 0, 0, i % nb)),
        out_shape=jax.ShapeDtypeStruct((N // T, N_HEADS, LANES, T), BF16),
        compiler_params=_params("arbitrary"),
        name="q_proj_t",
    )(h, g, wq_t, gq_col, ct, bound)


def _attn_prompt_kernel(q_ref, k_ref, v_ref, o_ref, *, blk, head_dim, bounded):
    qi = pl.program_id(2)
    n_h = q_ref.shape[0]
    q_t = [q_ref[x] for x in range(n_h)]
    key_iota = lax.broadcasted_iota(jnp.int32, (blk, blk), 0)
    qry_iota = lax.broadcasted_iota(jnp.int32, (blk, blk), 1)

    def step(j0, n_blocks, carry, diagonal_last):
        offs = [pl.multiple_of((j0 + b) * blk, blk) for b in range(n_blocks)]
        s_all = [[_dot(k_ref[pl.ds(off, blk), x * LANES:(x + 1) * LANES], q_t[x]) for x in range(n_h)]
                 for off in offs]
        out = []
        for x, (m, acc) in enumerate(carry):
            for b, off in enumerate(offs):
                s = s_all[b][x]
                if diagonal_last and b == n_blocks - 1:
                    s = jnp.where(key_iota <= qry_iota, s, NEG_BIG)
                if bounded:
                    acc = acc + _dot(v_ref[x, :, pl.ds(off, blk)], jnp.exp2(s).astype(BF16))
                    continue
                m_new = jnp.maximum(m, jnp.max(s, axis=0, keepdims=True))
                p = jnp.exp2(s - m_new).astype(BF16)
                acc = jnp.exp2(m - m_new) * acc + _dot(v_ref[x, :, pl.ds(off, blk)], p)
                m = m_new
            out.append((m, acc))
        return tuple(out)

    init = tuple((jnp.full((1, blk), NEG_BIG, F32), jnp.zeros((v_ref.shape[1], blk), F32)) for _ in range(n_h))
    n_wide = qi // ATT_UNROLL
    carry = lax.fori_loop(0, n_wide, lambda jj, c: step(jj * ATT_UNROLL, ATT_UNROLL, c, False), init)
    j0 = n_wide * ATT_UNROLL
    tails = [functools.partial(step, j0, r + 1, diagonal_last=True) for r in range(ATT_UNROLL)]
    for x, (_, acc) in enumerate(lax.switch(qi - j0, tails, carry)):
        o_ref[x] = (acc[:head_dim] / acc[head_dim:head_dim + 1]).astype(o_ref.dtype)


def _attn_prompt(q_t, k_aug, v_aug, bounded):
    B, H, _, T = q_t.shape
    vr = v_aug.shape[2]
    head_dim = vr - V_EXTRA
    blk = ATT_BLOCK
    hb = ATT_HEADS
    return pl.pallas_call(
        functools.partial(_attn_prompt_kernel, blk=blk, head_dim=head_dim, bounded=bounded),
        grid=(B, H // hb, T // blk),
        in_specs=[
            pl.BlockSpec((None, hb, LANES, blk), lambda b, h, i: (b, h, 0, i)),
            pl.BlockSpec((None, T, hb * LANES), lambda b, h, i: (b, 0, h)),
            pl.BlockSpec((None, hb, vr, T), lambda b, h, i: (b, h, 0, 0)),
        ],
        out_specs=pl.BlockSpec((None, hb, head_dim, blk), lambda b, h, i: (b, h, 0, i)),
        out_shape=jax.ShapeDtypeStruct((B, H, head_dim, T), BF16),
        compiler_params=_params("arbitrary", "arbitrary", "arbitrary"),
        name="attn_prompt_bounded" if bounded else "attn_prompt",
    )(q_t, k_aug, v_aug)


def _attn_sample_kernel(pt_ref, q_ref, lfn_ref, kn_ref, vn_ref, *refs, n_pages, page, ts, head_dim, new_pad):
    del pt_ref
    lf_refs = refs[:n_pages]
    k_refs = refs[n_pages:2 * n_pages]
    v_refs = refs[2 * n_pages:3 * n_pages]
    o_ref = refs[3 * n_pages]
    D = q_ref.shape[1]
    n_pairs = D // LANES
    tp = 8
    lane = lax.broadcasted_iota(jnp.int32, (tp, LANES), 1)
    first = lane < head_dim
    tri = (lax.broadcasted_iota(jnp.int32, (page, page), 0)
           <= lax.broadcasted_iota(jnp.int32, (page, page), 1)).astype(BF16)

    q = q_ref[...].astype(F32)
    q8 = jnp.concatenate([q, jnp.zeros((tp - ts, D), F32)], axis=0)
    q_pairs = []
    for p in range(n_pairs):
        qp = q8[:, p * LANES:(p + 1) * LANES]
        q_pairs.append(jnp.concatenate([jnp.where(first, qp, 0.0), jnp.where(first, 0.0, qp)], axis=0).astype(BF16))

    def running_sum(lf_t, carry):
        parts = _dot(jnp.concatenate(_split3(lf_t), axis=0), tri)
        return parts[:N_HEADS] + parts[N_HEADS:2 * N_HEADS] + parts[2 * N_HEADS:] + carry

    def head_rows(c):
        n = c.shape[1]
        return jnp.broadcast_to(c[:, None, :], (N_HEADS, tp, n)).reshape(N_HEADS * tp, n)

    carry = jnp.zeros((N_HEADS, 1), F32)
    s_pages = []
    for j in range(n_pages):
        c = running_sum(lf_refs[j][...], carry)
        carry = c[:, page - 1:]
        s = [_dot(q_pairs[p], k_refs[j][p * LANES:(p + 1) * LANES, :].astype(BF16)) for p in range(n_pairs)]
        s_pages.append(jnp.concatenate(s, axis=0) - head_rows(c))
    c = running_sum(lfn_ref[...], carry)[:, :new_pad]
    s = [_dot_nt(q_pairs[p], kn_ref[:, p * LANES:(p + 1) * LANES]) for p in range(n_pairs)]
    t_row = lax.broadcasted_iota(jnp.int32, (N_HEADS * tp, new_pad), 0) % tp
    t_key = lax.broadcasted_iota(jnp.int32, (N_HEADS * tp, new_pad), 1)
    s_new = jnp.where(t_key <= t_row, jnp.concatenate(s, axis=0) - head_rows(c), NEG_BIG)

    m_el = s_pages[0]
    for s in s_pages[1:]:
        m_el = jnp.maximum(m_el, s)
    m = jnp.maximum(jnp.max(m_el, axis=1, keepdims=True), jnp.max(s_new, axis=1, keepdims=True))
    l_el = jnp.zeros((N_HEADS * tp, page), F32)
    acc_t = jnp.zeros((D, N_HEADS * tp), F32)
    for j in range(n_pages):
        pj = jnp.exp(s_pages[j] - m)
        l_el = l_el + pj
        acc_t = acc_t + _dot(v_refs[j][...].astype(BF16), pj.T.astype(BF16))
    p_new = jnp.exp(s_new - m)
    l = jnp.sum(l_el, axis=1, keepdims=True) + jnp.sum(p_new, axis=1, keepdims=True)
    o = (acc_t.T + _dot(p_new.astype(BF16), vn_ref[...])) / l
    row_head = lax.broadcasted_iota(jnp.int32, (N_HEADS * tp, D), 0) // tp
    col_head = lax.broadcasted_iota(jnp.int32, (N_HEADS * tp, D), 1) // head_dim
    own = jnp.where(row_head == col_head, o, 0.0).reshape(N_HEADS, tp, D)
    o_ref[...] = jnp.sum(own, axis=0)[:ts].astype(o_ref.dtype)


def _attn_sample(q, lf_new_t, k_new, v_new, cache_k, cache_v, cache_logf, page_table):
    bs, ts, D = q.shape
    n_pages = page_table.shape[1]
    n_pool, page = cache_k.shape[0], cache_k.shape[1]
    assert page == LANES
    new_pad = k_new.shape[1]
    ck = jnp.transpose(cache_k, (0, 2, 3, 1)).reshape(n_pool, D, page)
    cv = jnp.transpose(cache_v, (0, 2, 3, 1)).reshape(n_pool, D, page)
    cl = jnp.transpose(cache_logf, (0, 2, 1))
    seq = lambda n: pl.BlockSpec((None, n, D), lambda b, pt: (b, 0, 0))
    paged = lambda r: [pl.BlockSpec((None, r, page), lambda b, pt, j=j: (pt[b, j], 0, 0)) for j in range(n_pages)]
    return pl.pallas_call(
        functools.partial(_attn_sample_kernel, n_pages=n_pages, page=page, ts=ts,
                          head_dim=D // N_HEADS, new_pad=new_pad),
        grid_spec=pltpu.PrefetchScalarGridSpec(
            num_scalar_prefetch=1,
            grid=(bs,),
            in_specs=[seq(ts), pl.BlockSpec((None, N_HEADS, page), lambda b, pt: (b, 0, 0)),
                      seq(new_pad), seq(new_pad)] + paged(N_HEADS) + paged(D) + paged(D),
            out_specs=seq(ts),
        ),
        out_shape=jax.ShapeDtypeStruct((bs, ts, D), BF16),
        compiler_params=_params("arbitrary"),
        name="attn_sample",
    )(page_table, q, lf_new_t, k_new, v_new, *([cl] * n_pages), *([ck] * n_pages), *([cv] * n_pages))


def kernel(x_prompt, x_sample, state_pool, cache_k, cache_v, cache_logf, page_table, g_pool, w_pool, pool_scale,
           g_attn, w_q, g_q, w_o, g_kv, w_k, w_v, g_k, w_f, b_f, g_mlp, w_up, w_down):
    Bp, Tp, D = x_prompt.shape
    Bs, Ts, _ = x_sample.shape
    n_a = g_pool.shape[0]
    n_b = g_attn.shape[0]
    head_dim = D // N_HEADS
    past = page_table.shape[1] * cache_k.shape[1]
    new_pad = 16

    row = lambda a: a.reshape(1, -1).astype(F32)
    w_pool_b, w_up_b, w_down_b, w_q_b, w_o_b = (w.astype(BF16) for w in (w_pool, w_up, w_down, w_q, w_o))
    w_k_b, w_v_b = w_k.astype(BF16), w_v.astype(BF16)
    w_qt_b = jnp.transpose(w_q_b, (0, 2, 1))
    w_f_p = jnp.pad(w_f, ((0, 0), (0, LANES - N_HEADS))).astype(BF16)
    b_f_p = jnp.pad(b_f.astype(F32), (0, LANES - N_HEADS)).reshape(1, LANES)
    e = (jnp.arange(D)[:, None] // head_dim == jnp.arange(LANES)[None, :]).astype(BF16)
    et = e.T
    g_k_t = row(jnp.tile(g_k, N_HEADS))

    h_p = x_prompt
    h_s = jnp.transpose(x_sample, (1, 0, 2))
    st_p, st_s = [], []
    for l in range(n_a):
        args = (row(g_pool[l]), w_pool_b, l, row(pool_scale[l]))
        h_p, tail = _pool_prompt(h_p, *args)
        st_p.append(tail[:, HALO - POOL_BUF:])
        h_p = _mlp(h_p.reshape(Bp * Tp, D), row(g_mlp[l]), w_up_b, w_down_b, l).reshape(Bp, Tp, D)
        h_s, st = _pool_sample(h_s, jnp.transpose(state_pool[l], (1, 0, 2)), *args, past)
        st_s.append(jnp.transpose(st, (1, 0, 2)))
        h_s = _mlp(h_s.reshape(Ts * Bs, D), row(g_mlp[l]), w_up_b, w_down_b, l).reshape(Ts, Bs, D)
    pool_state_prompt = jnp.stack(st_p, axis=0)
    pool_state_sample = jnp.stack(st_s, axis=0)

    h_p = h_p.reshape(Bp * Tp, D)
    h_s = jnp.transpose(h_s, (1, 0, 2)).reshape(Bs * Ts, D)

    g_k_col = jnp.tile(g_k, N_HEADS).astype(F32).reshape(N_HEADS, head_dim, 1)
    kt_p, vt_p, lft_p, ct_p, ka_p, va_p = _kv_prompt(h_p, Tp, row(g_kv), w_k_b.T, w_v_b.T, g_k_col, w_f_p, b_f_p)
    k_s, v_s, lf_s, kb_s, vb_s = _kv_sample(h_s, row(g_kv), w_k_b, w_v_b, g_k_t, w_f_p, b_f_p, e, et)
    pad_new = lambda a, n: jnp.pad(a.reshape(Bs, Ts, -1), ((0, 0), (0, n - Ts), (0, 0)))
    lfn_t = jnp.transpose(pad_new(lf_s, cache_logf.shape[1]), (0, 2, 1))
    kn, vn = pad_new(kb_s, new_pad), pad_new(vb_s, new_pad)
    ka_p = ka_p.reshape(Bp, Tp, N_HEADS * LANES)

    for l in range(n_b):
        j = n_a + l
        g_q_t = row(jnp.tile(g_q[l], N_HEADS))
        bound = (1.02 * LOG2E * head_dim ** 0.5) * jnp.max(jnp.abs(g_q[l])) * jnp.max(jnp.abs(g_k))
        g_q_col = (jnp.tile(g_q[l], N_HEADS) * (LOG2E * head_dim ** -0.5)).astype(F32).reshape(N_HEADS, head_dim, 1)
        q_p = _qproj_t(h_p, row(g_attn[l]), w_qt_b, l, g_q_col, ct_p, bound.reshape(1, 1).astype(F32))
        o_p = lax.cond(2.0 * bound < MAX_SELF_DEFICIT,
                       functools.partial(_attn_prompt, bounded=True),
                       functools.partial(_attn_prompt, bounded=False), q_p, ka_p, va_p).reshape(Bp, D, Tp)
        h_p = _mlp(h_p, row(g_mlp[j]), w_up_b, w_down_b, j, o_p, w_o_b, l, o_transposed=True)
        q_s = _qproj(h_s, row(g_attn[l]), w_q_b, l, g_q_t, e, et)
        o_s = _attn_sample(q_s.reshape(Bs, Ts, D), lfn_t, kn, vn, cache_k, cache_v, cache_logf, page_table)
        h_s = _mlp(h_s, row(g_mlp[j]), w_up_b, w_down_b, j, o_s.reshape(Bs * Ts, D), w_o_b, l)

    hd_t = lambda a: jnp.transpose(a.reshape(Bp, N_HEADS, head_dim, Tp), (0, 3, 1, 2))
    hd4 = lambda a: a.reshape(Bs, Ts, N_HEADS, head_dim)
    return (h_p.reshape(Bp, Tp, D), h_s.reshape(Bs, Ts, D), pool_state_prompt, pool_state_sample,
            hd_t(kt_p), hd_t(vt_p), jnp.transpose(lft_p, (0, 2, 1)).astype(cache_logf.dtype),
            hd4(k_s), hd4(v_s), lf_s.reshape(Bs, Ts, N_HEADS).astype(cache_logf.dtype))
```

```python
import functools

import jax
import jax.numpy as jnp
from jax import lax
from jax.experimental import pallas as pl
from jax.experimental.pallas import tpu as pltpu

F32 = jnp.float32
BF16 = jnp.bfloat16

EPS = 1e-6
POOL_WINDOWS = (2, 4, 8, 16)
POOL_BUF = max(POOL_WINDOWS) - 1
assert list(POOL_WINDOWS) == sorted(POOL_WINDOWS) and all(w & (w - 1) == 0 for w in POOL_WINDOWS)
N_HEADS = 16
LANES = 128
HALO = 16
VMEM_LIMIT = 56 * 1024 * 1024
NEG_BIG = -1e30

TOK_BLOCK = 512
FF_CHUNK = 1024
PROJ_SPLITS = 4
ATT_BLOCK = 512
ATT_UNROLL = 4
ATT_HEADS = 4
V_EXTRA = 16
LOG2E = 1.4426950408889634
MAX_SELF_DEFICIT = 100.0


def _dot(a, b):
    return jnp.dot(a, b, preferred_element_type=F32)


def _dot_nt(a, b):
    return lax.dot_general(a, b, (((1,), (1,)), ((), ())), preferred_element_type=F32)


def _rms(x, g):
    ms = jnp.mean(x * x, axis=-1, keepdims=True)
    return x * lax.rsqrt(ms + EPS) * g


def _split2(x):
    hi = x.astype(BF16)
    lo = (x - hi.astype(F32)).astype(BF16)
    return hi, lo


def _split3(x):
    hi = x.astype(BF16)
    r1 = x - hi.astype(F32)
    mid = r1.astype(BF16)
    lo = (r1 - mid.astype(F32)).astype(BF16)
    return hi, mid, lo


def _head_rms(x, g_tiled, e, et, head_dim):
    hi, lo = _split2(x * x)
    ss = _dot(hi, e) + _dot(lo, e)
    r = lax.rsqrt(ss * (1.0 / head_dim) + EPS)
    rh, rl = _split2(r)
    return x * (_dot(rh, et) + _dot(rl, et)) * g_tiled


def _proj_head_rms(x, w_ref, g_ref, e_ref, et_ref, head_dim):
    width = w_ref.shape[1] // PROJ_SPLITS
    cols = [slice(i * width, (i + 1) * width) for i in range(PROJ_SPLITS)]
    ys = [_dot(x, w_ref[:, c]) for c in cols]
    return jnp.concatenate([_head_rms(y, g_ref[:, c], e_ref[c, :], et_ref[:, c], head_dim)
                            for y, c in zip(ys, cols)], axis=1)


def _const_spec(shape):
    return pl.BlockSpec(shape, lambda *_: (0,) * len(shape), pipeline_mode=pl.Buffered(1))


def _layer_spec(shape, layer):
    return pl.BlockSpec((None,) + tuple(shape), lambda *_: (layer,) + (0,) * len(shape),
                        pipeline_mode=pl.Buffered(1))


def _params(*sem):
    return pltpu.CompilerParams(dimension_semantics=sem, vmem_limit_bytes=VMEM_LIMIT)


def _pool_mix_prompt(x, halo, i, g, w_ref, sc, tb, gdim):
    xn = _rms(x, g)
    hn = _rms(halo, g)
    sums = jnp.concatenate([jnp.where(i > 0, hn, 0.0), xn], axis=0)
    width, windows = 1, []
    for w in POOL_WINDOWS:
        while width < w:
            sums = sums + pltpu.roll(sums, width, 0)
            width *= 2
        windows.append(sums[HALO:, :gdim])
        sums = sums[:, gdim:]
    pos = i * tb + lax.broadcasted_iota(jnp.int32, (tb, 1), 0)
    out = []
    for gi, w in enumerate(POOL_WINDOWS):
        sl = slice(gi * gdim, (gi + 1) * gdim)
        cnt = jnp.minimum(pos + 1, w).astype(F32)
        pooled = windows[gi] / cnt - xn[:, sl]
        out.append(x[:, sl] + _dot(pooled.astype(BF16), w_ref[gi]) * sc[:, sl])
    return jnp.concatenate(out, axis=1), xn


def _pool_mlp_prompt_kernel(x_ref, halo_ref, gp_ref, wp_ref, sc_ref, gm_ref, wu_ref, wd_ref, o_ref, st_ref,
                            *, tb, gdim, n_chunks):
    h, xn = _pool_mix_prompt(x_ref[...], halo_ref[...], pl.program_id(1), gp_ref[...], wp_ref, sc_ref[...], tb, gdim)
    st_ref[...] = xn[tb - HALO:, :]
    o_ref[...] = _mlp_body(h, gm_ref[...], wu_ref, wd_ref, n_chunks)


def _pool_mlp_prompt(h, g_pool, w_pool, pool_layer, sc, g_mlp, w_up, w_down, layer):
    B, T, D = h.shape
    F = w_up.shape[2]
    tb = TOK_BLOCK
    gdim = D // len(POOL_WINDOWS)
    r = tb // HALO
    return pl.pallas_call(
        functools.partial(_pool_mlp_prompt_kernel, tb=tb, gdim=gdim, n_chunks=F // FF_CHUNK),
        grid=(B, T // tb),
        in_specs=[
            pl.BlockSpec((None, tb, D), lambda b, i: (b, i, 0)),
            pl.BlockSpec((None, HALO, D), lambda b, i: (b, jnp.maximum(i * r - 1, 0), 0)),
            _const_spec((1, D)),
            _layer_spec((len(POOL_WINDOWS), gdim, gdim), pool_layer),
            _const_spec((1, D)),
            _const_spec((1, D)), _layer_spec((D, F), layer), _layer_spec((F, D), layer),
        ],
        out_specs=[
            pl.BlockSpec((None, tb, D), lambda b, i: (b, i, 0)),
            pl.BlockSpec((None, HALO, D), lambda b, i: (b, 0, 0)),
        ],
        out_shape=[jax.ShapeDtypeStruct((B, T, D), F32), jax.ShapeDtypeStruct((B, HALO, D), F32)],
        compiler_params=_params("arbitrary", "arbitrary"),
        name="pool_mlp_prompt",
    )(h, h, g_pool, w_pool, sc, g_mlp, w_up, w_down)


def _pool_sample_kernel(x_ref, pf_ref, g_ref, w_ref, sc_ref, o_ref, st_ref, *, ts, bb, gdim, cnts):
    g = g_ref[...]
    sc = sc_ref[...]
    xs = [x_ref[t] for t in range(ts)]
    xn = [_rms(x, g) for x in xs]
    ext = [pf_ref[r] for r in range(POOL_BUF)] + xn
    for gi, w in enumerate(POOL_WINDOWS):
        sl = slice(gi * gdim, (gi + 1) * gdim)
        pooled = []
        for t in range(ts):
            acc = ext[POOL_BUF + t][:, sl]
            for j in range(1, w):
                acc = acc + ext[POOL_BUF + t - j][:, sl]
            pooled.append(acc / cnts[t][gi] - xn[t][:, sl])
        mixed = _dot(jnp.concatenate(pooled, axis=0).astype(BF16), w_ref[gi]) * sc[:, sl]
        for t in range(ts):
            o_ref[t, :, sl] = xs[t][:, sl] + mixed[t * bb:(t + 1) * bb]
    for r in range(POOL_BUF):
        st_ref[r] = ext[ts + r]


def _pool_sample(h_tm, prefix_tm, g, w, layer, sc, past):
    ts, bs, D = h_tm.shape
    bb = 32
    gdim = D // len(POOL_WINDOWS)
    cnts = tuple(tuple(float(min(past + t + 1, w)) for w in POOL_WINDOWS) for t in range(ts))
    return pl.pallas_call(
        functools.partial(_pool_sample_kernel, ts=ts, bb=bb, gdim=gdim, cnts=cnts),
        grid=(bs // bb,),
        in_specs=[
            pl.BlockSpec((ts, bb, D), lambda i: (0, i, 0)),
            pl.BlockSpec((POOL_BUF, bb, D), lambda i: (0, i, 0)),
            _const_spec((1, D)),
            _layer_spec((len(POOL_WINDOWS), gdim, gdim), layer),
            _const_spec((1, D)),
        ],
        out_specs=[
            pl.BlockSpec((ts, bb, D), lambda i: (0, i, 0)),
            pl.BlockSpec((POOL_BUF, bb, D), lambda i: (0, i, 0)),
        ],
        out_shape=[jax.ShapeDtypeStruct((ts, bs, D), F32), jax.ShapeDtypeStruct((POOL_BUF, bs, D), F32)],
        compiler_params=_params("arbitrary"),
        name="pool_sample",
    )(h_tm, prefix_tm, g, w, sc)


def _mlp_body(h, g, wu_ref, wd_ref, n_chunks):
    xn = _rms(h, g).astype(BF16)
    acc = h
    for c in range(n_chunks):
        cs = slice(c * FF_CHUNK, (c + 1) * FF_CHUNK)
        a = jnp.maximum(_dot(xn, wu_ref[:, cs]), 0.0)
        acc = acc + _dot((a * a).astype(BF16), wd_ref[cs, :])
    return acc


def _mlp_kernel(*refs, o_mode, n_chunks):
    if o_mode is None:
        h_ref, g_ref, wu_ref, wd_ref, out_ref = refs
        h = h_ref[...]
    else:
        h_ref, o_ref, wo_ref, g_ref, wu_ref, wd_ref, out_ref = refs
        o = o_ref[...]
        if o_mode == "transposed":
            proj = lax.dot_general(o, wo_ref[...], (((0,), (0,)), ((), ())), preferred_element_type=F32)
        else:
            proj = _dot(o, wo_ref[...])
        h = h_ref[...] + proj
    out_ref[...] = _mlp_body(h, g_ref[...], wu_ref, wd_ref, n_chunks)


def _mlp(h, g, w_up, w_down, layer, o=None, w_o=None, o_layer=None, o_transposed=False):
    N, D = h.shape
    F = w_up.shape[2]
    tm = TOK_BLOCK
    tok = pl.BlockSpec((tm, D), lambda i: (i, 0))
    in_specs = [tok]
    args = [h]
    o_mode = None
    if o is not None:
        if o_transposed:
            o_mode = "transposed"
            nb = o.shape[2] // tm
            in_specs.append(pl.BlockSpec((None, D, tm), lambda i: (i // nb, 0, i % nb)))
        else:
            o_mode = "rows"
            in_specs.append(tok)
        in_specs.append(_layer_spec((D, D), o_layer))
        args += [o, w_o]
    in_specs += [_const_spec((1, D)), _layer_spec((D, F), layer), _layer_spec((F, D), layer)]
    args += [g, w_up, w_down]
    return pl.pallas_call(
        functools.partial(_mlp_kernel, o_mode=o_mode, n_chunks=F // FF_CHUNK),
        grid=(N // tm,),
        in_specs=in_specs,
        out_specs=tok,
        out_shape=jax.ShapeDtypeStruct((N, D), F32),
        compiler_params=_params("arbitrary"),
        name="mlp" if o is None else "mlp_o",
    )(*args)


def _log_sigmoid(z):
    return -(jnp.maximum(-z, 0.0) + jnp.log1p(jnp.exp(-jnp.abs(z))))


def _gates(hn, wf_ref, bf_ref):
    return _log_sigmoid(_dot(hn, wf_ref[...]) + bf_ref[...])


def _kv_sample_kernel(h_ref, g_ref, wk_ref, wv_ref, gk_ref, wf_ref, bf_ref, e_ref, et_ref,
                      k_ref, v_ref, lf_ref, kb_ref, vb_ref, *, head_dim):
    hn = _rms(h_ref[...], g_ref[...]).astype(BF16)
    k = _proj_head_rms(hn, wk_ref, gk_ref, e_ref, et_ref, head_dim)
    v = _dot(hn, wv_ref[...])
    lf = _gates(hn, wf_ref, bf_ref)
    k_ref[...] = k
    v_ref[...] = v
    kb_ref[...] = k.astype(BF16)
    vb_ref[...] = v.astype(BF16)
    lf_ref[...] = lf[:, :N_HEADS]


def _kv_sample(h, g, wk, wv, gk_t, wf_p, bf_p, e, et):
    N, D = h.shape
    tm = TOK_BLOCK
    tok = pl.BlockSpec((tm, D), lambda i: (i, 0))
    return pl.pallas_call(
        functools.partial(_kv_sample_kernel, head_dim=D // N_HEADS),
        grid=(N // tm,),
        in_specs=[tok, _const_spec((1, D)), _const_spec((D, D)), _const_spec((D, D)), _const_spec((1, D)),
                  _const_spec((D, LANES)), _const_spec((1, LANES)), _const_spec((D, LANES)),
                  _const_spec((LANES, D))],
        out_specs=[tok, tok, pl.BlockSpec((tm, N_HEADS), lambda i: (i, 0)), tok, tok],
        out_shape=[jax.ShapeDtypeStruct((N, D), F32), jax.ShapeDtypeStruct((N, D), F32),
                   jax.ShapeDtypeStruct((N, N_HEADS), F32),
                   jax.ShapeDtypeStruct((N, D), BF16), jax.ShapeDtypeStruct((N, D), BF16)],
        compiler_params=_params("arbitrary"),
        name="shared_kv_sample",
    )(h, g, wk, wv, gk_t, wf_p, bf_p, e, et)


def _kv_prompt_kernel(h_ref, g_ref, wkt_ref, wvt_ref, gk_ref, wf_ref, bf_ref, tril_ref, spread_ref,
                      kt_ref, vt_ref, lft_ref, ct_ref, ka_ref, va_ref, carry_ref, *, nb, head_dim):
    i = pl.program_id(0)
    hn = _rms(h_ref[...], g_ref[...]).astype(BF16)
    tm = hn.shape[0]
    y = _dot_nt(wkt_ref[...], hn).reshape(N_HEADS, head_dim, tm)
    v_t = _dot_nt(wvt_ref[...], hn)
    lf = _gates(hn, wf_ref, bf_ref)
    k_t = (y * lax.rsqrt(jnp.mean(y * y, axis=1, keepdims=True) + EPS) * gk_ref[...]).reshape(N_HEADS * head_dim, tm)
    k = k_t.T
    lane = lax.broadcasted_iota(jnp.int32, (tm, LANES), 1)
    lf = jnp.where(lane < N_HEADS, lf, 0.0)
    kt_ref[...] = k_t
    vt_ref[...] = v_t
    lft_ref[...] = lf.T[:N_HEADS]

    @pl.when(i % nb == 0)
    def _():
        carry_ref[...] = jnp.zeros_like(carry_ref)

    hi, mid, lo = (t.astype(F32) for t in _split3(lf))
    sums = _dot(tril_ref[...], (hi + pltpu.roll(mid, N_HEADS, 1) + pltpu.roll(lo, 2 * N_HEADS, 1)).astype(BF16))
    sums = sums + pltpu.roll(sums, LANES - N_HEADS, 1) + pltpu.roll(sums, LANES - 2 * N_HEADS, 1)
    c = jnp.where(lane < N_HEADS, sums, 0.0) + carry_ref[...]
    carry_ref[...] = c[tm - 1:, :]
    ct_ref[...] = c.T[:N_HEADS]

    hi, mid, lo = (t.astype(F32) for t in _split3(c * -LOG2E))
    terms = hi + pltpu.roll(mid, N_HEADS, 1) + pltpu.roll(lo, 2 * N_HEADS, 1)
    ones = jnp.where((lane >= head_dim + 3) & (lane < head_dim + 6), 1.0, 0.0)
    aug = _dot(terms.astype(BF16), spread_ref[...])
    low = lane < head_dim
    for p in range(k.shape[1] // LANES):
        kp = k[:, p * LANES:(p + 1) * LANES]
        even = jnp.where(low, kp, ones) + aug[:, (2 * p) * LANES:(2 * p + 1) * LANES]
        odd = jnp.where(low, pltpu.roll(kp, head_dim, 1), ones) + aug[:, (2 * p + 1) * LANES:(2 * p + 2) * LANES]
        ka_ref[:, (2 * p) * LANES:(2 * p + 1) * LANES] = even.astype(BF16)
        ka_ref[:, (2 * p + 1) * LANES:(2 * p + 2) * LANES] = odd.astype(BF16)

    va_ref[:, 0:head_dim, :] = v_t.reshape(N_HEADS, head_dim, tm).astype(BF16)
    sub = lax.broadcasted_iota(jnp.int32, (N_HEADS, V_EXTRA, tm), 1)
    va_ref[:, head_dim:, :] = jnp.where(sub == 0, 1.0, 0.0).astype(BF16)


def _kv_prompt(h, T, g, wk_t, wv_t, gk_col, wf_p, bf_p):
    N, D = h.shape
    B = N // T
    head_dim = D // N_HEADS
    tm = TOK_BLOCK
    nb = T // tm
    tok = pl.BlockSpec((tm, D), lambda i: (i, 0))
    tril = (jnp.arange(tm)[:, None] >= jnp.arange(tm)[None, :]).astype(BF16)
    src = jnp.arange(LANES)[:, None]
    dst = jnp.arange(N_HEADS * LANES)[None, :]
    spread = ((src < 3 * N_HEADS) & (dst // LANES == src % N_HEADS)
              & (dst % LANES == head_dim + src // N_HEADS)).astype(BF16)
    seq_t = lambda r: pl.BlockSpec((None, r, tm), lambda i: (i // nb, 0, i % nb))
    return pl.pallas_call(
        functools.partial(_kv_prompt_kernel, nb=nb, head_dim=head_dim),
        grid=(N // tm,),
        in_specs=[tok, _const_spec((1, D)), _const_spec((D, D)), _const_spec((D, D)),
                  _const_spec((N_HEADS, head_dim, 1)), _const_spec((D, LANES)), _const_spec((1, LANES)),
                  _const_spec((tm, tm)), _const_spec((LANES, N_HEADS * LANES))],
        out_specs=[seq_t(D), seq_t(D), seq_t(N_HEADS), seq_t(N_HEADS),
                   pl.BlockSpec((tm, N_HEADS * LANES), lambda i: (i, 0)),
                   pl.BlockSpec((None, N_HEADS, head_dim + V_EXTRA, tm), lambda i: (i // nb, 0, 0, i % nb))],
        out_shape=[jax.ShapeDtypeStruct((B, D, T), F32), jax.ShapeDtypeStruct((B, D, T), F32),
                   jax.ShapeDtypeStruct((B, N_HEADS, T), F32), jax.ShapeDtypeStruct((B, N_HEADS, T), F32),
                   jax.ShapeDtypeStruct((N, N_HEADS * LANES), BF16),
                   jax.ShapeDtypeStruct((B, N_HEADS, head_dim + V_EXTRA, T), BF16)],
        scratch_shapes=[pltpu.VMEM((1, LANES), F32)],
        compiler_params=_params("arbitrary"),
        name="shared_kv_prompt",
    )(h, g, wk_t, wv_t, gk_col, wf_p, bf_p, tril, spread)


def _q_kernel(h_ref, g_ref, wq_ref, gq_ref, e_ref, et_ref, q_ref, *, head_dim):
    xn = _rms(h_ref[...], g_ref[...]).astype(BF16)
    q = _proj_head_rms(xn, wq_ref, gq_ref, e_ref, et_ref, head_dim)
    q_ref[...] = (q * (head_dim ** -0.5)).astype(BF16)


def _qproj(h, g, wq, layer, gq_t, e, et):
    N, D = h.shape
    tm = TOK_BLOCK
    tok = pl.BlockSpec((tm, D), lambda i: (i, 0))
    return pl.pallas_call(
        functools.partial(_q_kernel, head_dim=D // N_HEADS),
        grid=(N // tm,),
        in_specs=[tok, _const_spec((1, D)), _layer_spec((D, D), layer), _const_spec((1, D)),
                  _const_spec((D, LANES)), _const_spec((LANES, D))],
        out_specs=tok,
        out_shape=jax.ShapeDtypeStruct((N, D), BF16),
        compiler_params=_params("arbitrary"),
        name="q_proj",
    )(h, g, wq, gq_t, e, et)


def _q_t_kernel(h_ref, g_ref, wqt_ref, gq_ref, ct_ref, bound_ref, q_ref, *, head_dim):
    xn = _rms(h_ref[...], g_ref[...]).astype(BF16)
    tm = xn.shape[0]
    y = _dot_nt(wqt_ref[...], xn).reshape(N_HEADS, head_dim, tm)
    r = lax.rsqrt(jnp.mean(y * y, axis=1, keepdims=True) + EPS)
    q_ref[:, 0:head_dim, :] = (y * r * gq_ref[...]).astype(BF16)
    shape = (N_HEADS, LANES - head_dim, tm)
    sub = lax.broadcasted_iota(jnp.int32, shape, 1)
    hi, mid, lo = (jnp.broadcast_to(t.astype(F32)[:, None, :], shape)
                   for t in _split3(ct_ref[...] * LOG2E - bound_ref[...]))
    tail = jnp.where(sub < 3, 1.0, jnp.where(sub == 3, hi, jnp.where(sub == 4, mid, jnp.where(sub == 5, lo, 0.0))))
    q_ref[:, head_dim:, :] = tail.astype(BF16)


def _qproj_t(h, g, wq_t, layer, gq_col, ct, bound):
    N, D = h.shape
    tm = TOK_BLOCK
    T = ct.shape[2]
    nb = T // tm
    head_dim = D // N_HEADS
    return pl.pallas_call(
        functools.partial(_q_t_kernel, head_dim=head_dim),
        grid=(N // tm,),
        in_specs=[pl.BlockSpec((tm, D), lambda i: (i, 0)), _const_spec((1, D)), _layer_spec((D, D), layer),
                  _const_spec((N_HEADS, head_dim, 1)),
                  pl.BlockSpec((None, N_HEADS, tm), lambda i: (i // nb, 0, i % nb)), _const_spec((1, 1))],
        out_specs=pl.BlockSpec((None, N_HEADS, LANES, tm), lambda i: (i // nb, 0, 0, i % nb)),
        out_shape=jax.ShapeDtypeStruct((N // T, N_HEADS, LANES, T), BF16),
        compiler_params=_params("arbitrary"),
        name="q_proj_t",
    )(h, g, wq_t, gq_col, ct, bound)


def _attn_prompt_kernel(q_ref, k_ref, v_ref, o_ref, *, blk, head_dim, bounded):
    qi = pl.program_id(2)
    n_h = q_ref.shape[0]
    q_t = [q_ref[x] for x in range(n_h)]
    key_iota = lax.broadcasted_iota(jnp.int32, (blk, blk), 0)
    qry_iota = lax.broadcasted_iota(jnp.int32, (blk, blk), 1)

    def step(j0, n_blocks, carry, diagonal_last):
        offs = [pl.multiple_of((j0 + b) * blk, blk) for b in range(n_blocks)]
        s_all = [[_dot(k_ref[pl.ds(off, blk), x * LANES:(x + 1) * LANES], q_t[x]) for x in range(n_h)]
                 for off in offs]
        out = []
        for x, (m, acc) in enumerate(carry):
            for b, off in enumerate(offs):
                s = s_all[b][x]
                if diagonal_last and b == n_blocks - 1:
                    s = jnp.where(key_iota <= qry_iota, s, NEG_BIG)
                if bounded:
                    acc = acc + _dot(v_ref[x, :, pl.ds(off, blk)], jnp.exp2(s).astype(BF16))
                    continue
                m_new = jnp.maximum(m, jnp.max(s, axis=0, keepdims=True))
                p = jnp.exp2(s - m_new).astype(BF16)
                acc = jnp.exp2(m - m_new) * acc + _dot(v_ref[x, :, pl.ds(off, blk)], p)
                m = m_new
            out.append((m, acc))
        return tuple(out)

    init = tuple((jnp.full((1, blk), NEG_BIG, F32), jnp.zeros((v_ref.shape[1], blk), F32)) for _ in range(n_h))
    n_wide = qi // ATT_UNROLL
    carry = lax.fori_loop(0, n_wide, lambda jj, c: step(jj * ATT_UNROLL, ATT_UNROLL, c, False), init)
    j0 = n_wide * ATT_UNROLL
    tails = [functools.partial(step, j0, r + 1, diagonal_last=True) for r in range(ATT_UNROLL)]
    for x, (_, acc) in enumerate(lax.switch(qi - j0, tails, carry)):
        o_ref[x] = (acc[:head_dim] / acc[head_dim:head_dim + 1]).astype(o_ref.dtype)


def _attn_prompt(q_t, k_aug, v_aug, bounded):
    B, H, _, T = q_t.shape
    vr = v_aug.shape[2]
    head_dim = vr - V_EXTRA
    blk = ATT_BLOCK
    hb = ATT_HEADS
    return pl.pallas_call(
        functools.partial(_attn_prompt_kernel, blk=blk, head_dim=head_dim, bounded=bounded),
        grid=(B, H // hb, T // blk),
        in_specs=[
            pl.BlockSpec((None, hb, LANES, blk), lambda b, h, i: (b, h, 0, i)),
            pl.BlockSpec((None, T, hb * LANES), lambda b, h, i: (b, 0, h)),
            pl.BlockSpec((None, hb, vr, T), lambda b, h, i: (b, h, 0, 0)),
        ],
        out_specs=pl.BlockSpec((None, hb, head_dim, blk), lambda b, h, i: (b, h, 0, i)),
        out_shape=jax.ShapeDtypeStruct((B, H, head_dim, T), BF16),
        compiler_params=_params("arbitrary", "arbitrary", "arbitrary"),
        name="attn_prompt_bounded" if bounded else "attn_prompt",
    )(q_t, k_aug, v_aug)


def _attn_sample_kernel(pt_ref, q_ref, lfn_ref, kn_ref, vn_ref, *refs, n_pages, page, ts, head_dim, new_pad):
    del pt_ref
    lf_refs = refs[:n_pages]
    k_refs = refs[n_pages:2 * n_pages]
    v_refs = refs[2 * n_pages:3 * n_pages]
    o_ref = refs[3 * n_pages]
    D = q_ref.shape[1]
    n_pairs = D // LANES
    tp = 8
    lane = lax.broadcasted_iota(jnp.int32, (tp, LANES), 1)
    first = lane < head_dim
    tri = (lax.broadcasted_iota(jnp.int32, (page, page), 0)
           <= lax.broadcasted_iota(jnp.int32, (page, page), 1)).astype(BF16)

    q = q_ref[...].astype(F32)
    q8 = jnp.concatenate([q, jnp.zeros((tp - ts, D), F32)], axis=0)
    q_pairs = []
    for p in range(n_pairs):
        qp = q8[:, p * LANES:(p + 1) * LANES]
        q_pairs.append(jnp.concatenate([jnp.where(first, qp, 0.0), jnp.where(first, 0.0, qp)], axis=0).astype(BF16))

    def running_sum(lf_t, carry):
        parts = _dot(jnp.concatenate(_split3(lf_t), axis=0), tri)
        return parts[:N_HEADS] + parts[N_HEADS:2 * N_HEADS] + parts[2 * N_HEADS:] + carry

    def head_rows(c):
        n = c.shape[1]
        return jnp.broadcast_to(c[:, None, :], (N_HEADS, tp, n)).reshape(N_HEADS * tp, n)

    carry = jnp.zeros((N_HEADS, 1), F32)
    s_pages = []
    for j in range(n_pages):
        c = running_sum(lf_refs[j][...], carry)
        carry = c[:, page - 1:]
        s = [_dot(q_pairs[p], k_refs[j][p * LANES:(p + 1) * LANES, :].astype(BF16)) for p in range(n_pairs)]
        s_pages.append(jnp.concatenate(s, axis=0) - head_rows(c))
    c = running_sum(lfn_ref[...], carry)[:, :new_pad]
    s = [_dot_nt(q_pairs[p], kn_ref[:, p * LANES:(p + 1) * LANES]) for p in range(n_pairs)]
    t_row = lax.broadcasted_iota(jnp.int32, (N_HEADS * tp, new_pad), 0) % tp
    t_key = lax.broadcasted_iota(jnp.int32, (N_HEADS * tp, new_pad), 1)
    s_new = jnp.where(t_key <= t_row, jnp.concatenate(s, axis=0) - head_rows(c), NEG_BIG)

    m_el = s_pages[0]
    for s in s_pages[1:]:
        m_el = jnp.maximum(m_el, s)
    m = jnp.maximum(jnp.max(m_el, axis=1, keepdims=True), jnp.max(s_new, axis=1, keepdims=True))
    l_el = jnp.zeros((N_HEADS * tp, page), F32)
    acc_t = jnp.zeros((D, N_HEADS * tp), F32)
    for j in range(n_pages):
        pj = jnp.exp(s_pages[j] - m)
        l_el = l_el + pj
        acc_t = acc_t + _dot(v_refs[j][...].astype(BF16), pj.T.astype(BF16))
    p_new = jnp.exp(s_new - m)
    l = jnp.sum(l_el, axis=1, keepdims=True) + jnp.sum(p_new, axis=1, keepdims=True)
    o = (acc_t.T + _dot(p_new.astype(BF16), vn_ref[...])) / l
    row_head = lax.broadcasted_iota(jnp.int32, (N_HEADS * tp, D), 0) // tp
    col_head = lax.broadcasted_iota(jnp.int32, (N_HEADS * tp, D), 1) // head_dim
    own = jnp.where(row_head == col_head, o, 0.0).reshape(N_HEADS, tp, D)
    o_ref[...] = jnp.sum(own, axis=0)[:ts].astype(o_ref.dtype)


def _attn_sample(q, lf_new_t, k_new, v_new, cache_k, cache_v, cache_logf, page_table):
    bs, ts, D = q.shape
    n_pages = page_table.shape[1]
    n_pool, page = cache_k.shape[0], cache_k.shape[1]
    assert page == LANES
    new_pad = k_new.shape[1]
    ck = jnp.transpose(cache_k, (0, 2, 3, 1)).reshape(n_pool, D, page)
    cv = jnp.transpose(cache_v, (0, 2, 3, 1)).reshape(n_pool, D, page)
    cl = jnp.transpose(cache_logf, (0, 2, 1))
    seq = lambda n: pl.BlockSpec((None, n, D), lambda b, pt: (b, 0, 0))
    paged = lambda r: [pl.BlockSpec((None, r, page), lambda b, pt, j=j: (pt[b, j], 0, 0)) for j in range(n_pages)]
    return pl.pallas_call(
        functools.partial(_attn_sample_kernel, n_pages=n_pages, page=page, ts=ts,
                          head_dim=D // N_HEADS, new_pad=new_pad),
        grid_spec=pltpu.PrefetchScalarGridSpec(
            num_scalar_prefetch=1,
            grid=(bs,),
            in_specs=[seq(ts), pl.BlockSpec((None, N_HEADS, page), lambda b, pt: (b, 0, 0)),
                      seq(new_pad), seq(new_pad)] + paged(N_HEADS) + paged(D) + paged(D),
            out_specs=seq(ts),
        ),
        out_shape=jax.ShapeDtypeStruct((bs, ts, D), BF16),
        compiler_params=_params("arbitrary"),
        name="attn_sample",
    )(page_table, q, lf_new_t, k_new, v_new, *([cl] * n_pages), *([ck] * n_pages), *([cv] * n_pages))


def kernel(x_prompt, x_sample, state_pool, cache_k, cache_v, cache_logf, page_table, g_pool, w_pool, pool_scale,
           g_attn, w_q, g_q, w_o, g_kv, w_k, w_v, g_k, w_f, b_f, g_mlp, w_up, w_down):
    Bp, Tp, D = x_prompt.shape
    Bs, Ts, _ = x_sample.shape
    n_a = g_pool.shape[0]
    n_b = g_attn.shape[0]
    head_dim = D // N_HEADS
    past = page_table.shape[1] * cache_k.shape[1]
    new_pad = 16

    row = lambda a: a.reshape(1, -1).astype(F32)
    w_pool_b, w_up_b, w_down_b, w_q_b, w_o_b = (w.astype(BF16) for w in (w_pool, w_up, w_down, w_q, w_o))
    w_k_b, w_v_b = w_k.astype(BF16), w_v.astype(BF16)
    w_qt_b = jnp.transpose(w_q_b, (0, 2, 1))
    w_f_p = jnp.pad(w_f, ((0, 0), (0, LANES - N_HEADS))).astype(BF16)
    b_f_p = jnp.pad(b_f.astype(F32), (0, LANES - N_HEADS)).reshape(1, LANES)
    e = (jnp.arange(D)[:, None] // head_dim == jnp.arange(LANES)[None, :]).astype(BF16)
    et = e.T
    g_k_t = row(jnp.tile(g_k, N_HEADS))

    h_p = x_prompt
    h_s = jnp.transpose(x_sample, (1, 0, 2))
    st_p, st_s = [], []
    for l in range(n_a):
        args = (row(g_pool[l]), w_pool_b, l, row(pool_scale[l]))
        h_p, tail = _pool_mlp_prompt(h_p, *args, row(g_mlp[l]), w_up_b, w_down_b, l)
        st_p.append(tail[:, HALO - POOL_BUF:])
        h_s, st = _pool_sample(h_s, jnp.transpose(state_pool[l], (1, 0, 2)), *args, past)
        st_s.append(jnp.transpose(st, (1, 0, 2)))
        h_s = _mlp(h_s.reshape(Ts * Bs, D), row(g_mlp[l]), w_up_b, w_down_b, l).reshape(Ts, Bs, D)
    pool_state_prompt = jnp.stack(st_p, axis=0)
    pool_state_sample = jnp.stack(st_s, axis=0)

    h_p = h_p.reshape(Bp * Tp, D)
    h_s = jnp.transpose(h_s, (1, 0, 2)).reshape(Bs * Ts, D)

    g_k_col = jnp.tile(g_k, N_HEADS).astype(F32).reshape(N_HEADS, head_dim, 1)
    kt_p, vt_p, lft_p, ct_p, ka_p, va_p = _kv_prompt(h_p, Tp, row(g_kv), w_k_b.T, w_v_b.T, g_k_col, w_f_p, b_f_p)
    k_s, v_s, lf_s, kb_s, vb_s = _kv_sample(h_s, row(g_kv), w_k_b, w_v_b, g_k_t, w_f_p, b_f_p, e, et)
    pad_new = lambda a, n: jnp.pad(a.reshape(Bs, Ts, -1), ((0, 0), (0, n - Ts), (0, 0)))
    lfn_t = jnp.transpose(pad_new(lf_s, cache_logf.shape[1]), (0, 2, 1))
    kn, vn = pad_new(kb_s, new_pad), pad_new(vb_s, new_pad)
    ka_p = ka_p.reshape(Bp, Tp, N_HEADS * LANES)

    for l in range(n_b):
        j = n_a + l
        g_q_t = row(jnp.tile(g_q[l], N_HEADS))
        bound = (1.02 * LOG2E * head_dim ** 0.5) * jnp.max(jnp.abs(g_q[l])) * jnp.max(jnp.abs(g_k))
        g_q_col = (jnp.tile(g_q[l], N_HEADS) * (LOG2E * head_dim ** -0.5)).astype(F32).reshape(N_HEADS, head_dim, 1)
        q_p = _qproj_t(h_p, row(g_attn[l]), w_qt_b, l, g_q_col, ct_p, bound.reshape(1, 1).astype(F32))
        o_p = lax.cond(2.0 * bound < MAX_SELF_DEFICIT,
                       functools.partial(_attn_prompt, bounded=True),
                       functools.partial(_attn_prompt, bounded=False), q_p, ka_p, va_p).reshape(Bp, D, Tp)
        h_p = _mlp(h_p, row(g_mlp[j]), w_up_b, w_down_b, j, o_p, w_o_b, l, o_transposed=True)
        q_s = _qproj(h_s, row(g_attn[l]), w_q_b, l, g_q_t, e, et)
        o_s = _attn_sample(q_s.reshape(Bs, Ts, D), lfn_t, kn, vn, cache_k, cache_v, cache_logf, page_table)
        h_s = _mlp(h_s, row(g_mlp[j]), w_up_b, w_down_b, j, o_s.reshape(Bs * Ts, D), w_o_b, l)

    hd_t = lambda a: jnp.transpose(a.reshape(Bp, N_HEADS, head_dim, Tp), (0, 3, 1, 2))
    hd4 = lambda a: a.reshape(Bs, Ts, N_HEADS, head_dim)
    return (h_p.reshape(Bp, Tp, D), h_s.reshape(Bs, Ts, D), pool_state_prompt, pool_state_sample,
            hd_t(kt_p), hd_t(vt_p), jnp.transpose(lft_p, (0, 2, 1)).astype(cache_logf.dtype),
            hd4(k_s), hd4(v_s), lf_s.reshape(Bs, Ts, N_HEADS).astype(cache_logf.dtype))
```

```python
import functools

import jax
import jax.numpy as jnp
from jax import lax
from jax.experimental import pallas as pl
from jax.experimental.pallas import tpu as pltpu

F32 = jnp.float32
BF16 = jnp.bfloat16

EPS = 1e-6
POOL_WINDOWS = (2, 4, 8, 16)
POOL_BUF = max(POOL_WINDOWS) - 1
assert list(POOL_WINDOWS) == sorted(POOL_WINDOWS) and all(w & (w - 1) == 0 for w in POOL_WINDOWS)
N_HEADS = 16
LANES = 128
HALO = 16
VMEM_LIMIT = 56 * 1024 * 1024
NEG_BIG = -1e30

TOK_BLOCK = 512
FF_CHUNK = 1024
PROJ_SPLITS = 4
ATT_BLOCK = 512
ATT_UNROLL = 2
ATT_HEADS = 4
V_EXTRA = 16
LOG2E = 1.4426950408889634
MAX_SELF_DEFICIT = 100.0


def _dot(a, b):
    return jnp.dot(a, b, preferred_element_type=F32)


def _dot_nt(a, b):
    return lax.dot_general(a, b, (((1,), (1,)), ((), ())), preferred_element_type=F32)


def _rms(x, g):
    ms = jnp.mean(x * x, axis=-1, keepdims=True)
    return x * lax.rsqrt(ms + EPS) * g


def _split2(x):
    hi = x.astype(BF16)
    lo = (x - hi.astype(F32)).astype(BF16)
    return hi, lo


def _split3(x):
    hi = x.astype(BF16)
    r1 = x - hi.astype(F32)
    mid = r1.astype(BF16)
    lo = (r1 - mid.astype(F32)).astype(BF16)
    return hi, mid, lo


def _head_rms(x, g_tiled, e, et, head_dim):
    hi, lo = _split2(x * x)
    ss = _dot(hi, e) + _dot(lo, e)
    r = lax.rsqrt(ss * (1.0 / head_dim) + EPS)
    rh, rl = _split2(r)
    return x * (_dot(rh, et) + _dot(rl, et)) * g_tiled


def _proj_head_rms(x, w_ref, g_ref, e_ref, et_ref, head_dim):
    width = w_ref.shape[1] // PROJ_SPLITS
    cols = [slice(i * width, (i + 1) * width) for i in range(PROJ_SPLITS)]
    ys = [_dot(x, w_ref[:, c]) for c in cols]
    return jnp.concatenate([_head_rms(y, g_ref[:, c], e_ref[c, :], et_ref[:, c], head_dim)
                            for y, c in zip(ys, cols)], axis=1)


def _const_spec(shape):
    return pl.BlockSpec(shape, lambda *_: (0,) * len(shape), pipeline_mode=pl.Buffered(1))


def _layer_spec(shape, layer):
    return pl.BlockSpec((None,) + tuple(shape), lambda *_: (layer,) + (0,) * len(shape),
                        pipeline_mode=pl.Buffered(1))


def _params(*sem):
    return pltpu.CompilerParams(dimension_semantics=sem, vmem_limit_bytes=VMEM_LIMIT)


def _pool_mix_prompt(x, halo, i, g, w_ref, sc, tb, gdim):
    xn = _rms(x, g)
    hn = _rms(halo, g)
    sums = jnp.concatenate([jnp.where(i > 0, hn, 0.0), xn], axis=0)
    width, windows = 1, []
    for w in POOL_WINDOWS:
        while width < w:
            sums = sums + pltpu.roll(sums, width, 0)
            width *= 2
        windows.append(sums[HALO:, :gdim])
        sums = sums[:, gdim:]
    pos = i * tb + lax.broadcasted_iota(jnp.int32, (tb, 1), 0)
    out = []
    for gi, w in enumerate(POOL_WINDOWS):
        sl = slice(gi * gdim, (gi + 1) * gdim)
        cnt = jnp.minimum(pos + 1, w).astype(F32)
        pooled = windows[gi] / cnt - xn[:, sl]
        out.append(x[:, sl] + _dot(pooled.astype(BF16), w_ref[gi]) * sc[:, sl])
    return jnp.concatenate(out, axis=1), xn


def _pool_mlp_prompt_kernel(x_ref, halo_ref, gp_ref, wp_ref, sc_ref, gm_ref, wu_ref, wd_ref, o_ref, st_ref,
                            *, tb, gdim, n_chunks):
    h, xn = _pool_mix_prompt(x_ref[...], halo_ref[...], pl.program_id(1), gp_ref[...], wp_ref, sc_ref[...], tb, gdim)
    st_ref[...] = xn[tb - HALO:, :]
    o_ref[...] = _mlp_body(h, gm_ref[...], wu_ref, wd_ref, n_chunks)


def _pool_mlp_prompt(h, g_pool, w_pool, pool_layer, sc, g_mlp, w_up, w_down, layer):
    B, T, D = h.shape
    F = w_up.shape[2]
    tb = TOK_BLOCK
    gdim = D // len(POOL_WINDOWS)
    r = tb // HALO
    return pl.pallas_call(
        functools.partial(_pool_mlp_prompt_kernel, tb=tb, gdim=gdim, n_chunks=F // FF_CHUNK),
        grid=(B, T // tb),
        in_specs=[
            pl.BlockSpec((None, tb, D), lambda b, i: (b, i, 0)),
            pl.BlockSpec((None, HALO, D), lambda b, i: (b, jnp.maximum(i * r - 1, 0), 0)),
            _const_spec((1, D)),
            _layer_spec((len(POOL_WINDOWS), gdim, gdim), pool_layer),
            _const_spec((1, D)),
            _const_spec((1, D)), _layer_spec((D, F), layer), _layer_spec((F, D), layer),
        ],
        out_specs=[
            pl.BlockSpec((None, tb, D), lambda b, i: (b, i, 0)),
            pl.BlockSpec((None, HALO, D), lambda b, i: (b, 0, 0)),
        ],
        out_shape=[jax.ShapeDtypeStruct((B, T, D), F32), jax.ShapeDtypeStruct((B, HALO, D), F32)],
        compiler_params=_params("arbitrary", "arbitrary"),
        name="pool_mlp_prompt",
    )(h, h, g_pool, w_pool, sc, g_mlp, w_up, w_down)


def _pool_sample_kernel(x_ref, pf_ref, g_ref, w_ref, sc_ref, o_ref, st_ref, *, ts, bb, gdim, cnts):
    g = g_ref[...]
    sc = sc_ref[...]
    xs = [x_ref[t] for t in range(ts)]
    xn = [_rms(x, g) for x in xs]
    ext = [pf_ref[r] for r in range(POOL_BUF)] + xn
    for gi, w in enumerate(POOL_WINDOWS):
        sl = slice(gi * gdim, (gi + 1) * gdim)
        pooled = []
        for t in range(ts):
            acc = ext[POOL_BUF + t][:, sl]
            for j in range(1, w):
                acc = acc + ext[POOL_BUF + t - j][:, sl]
            pooled.append(acc / cnts[t][gi] - xn[t][:, sl])
        mixed = _dot(jnp.concatenate(pooled, axis=0).astype(BF16), w_ref[gi]) * sc[:, sl]
        for t in range(ts):
            o_ref[t, :, sl] = xs[t][:, sl] + mixed[t * bb:(t + 1) * bb]
    for r in range(POOL_BUF):
        st_ref[r] = ext[ts + r]


def _pool_sample(h_tm, prefix_tm, g, w, layer, sc, past):
    ts, bs, D = h_tm.shape
    bb = 32
    gdim = D // len(POOL_WINDOWS)
    cnts = tuple(tuple(float(min(past + t + 1, w)) for w in POOL_WINDOWS) for t in range(ts))
    return pl.pallas_call(
        functools.partial(_pool_sample_kernel, ts=ts, bb=bb, gdim=gdim, cnts=cnts),
        grid=(bs // bb,),
        in_specs=[
            pl.BlockSpec((ts, bb, D), lambda i: (0, i, 0)),
            pl.BlockSpec((POOL_BUF, bb, D), lambda i: (0, i, 0)),
            _const_spec((1, D)),
            _layer_spec((len(POOL_WINDOWS), gdim, gdim), layer),
            _const_spec((1, D)),
        ],
        out_specs=[
            pl.BlockSpec((ts, bb, D), lambda i: (0, i, 0)),
            pl.BlockSpec((POOL_BUF, bb, D), lambda i: (0, i, 0)),
        ],
        out_shape=[jax.ShapeDtypeStruct((ts, bs, D), F32), jax.ShapeDtypeStruct((POOL_BUF, bs, D), F32)],
        compiler_params=_params("arbitrary"),
        name="pool_sample",
    )(h_tm, prefix_tm, g, w, sc)


def _mlp_body(h, g, wu_ref, wd_ref, n_chunks):
    xn = _rms(h, g).astype(BF16)
    acc = h
    for c in range(n_chunks):
        cs = slice(c * FF_CHUNK, (c + 1) * FF_CHUNK)
        a = jnp.maximum(_dot(xn, wu_ref[:, cs]), 0.0)
        acc = acc + _dot((a * a).astype(BF16), wd_ref[cs, :])
    return acc


def _mlp_kernel(*refs, o_mode, n_chunks):
    if o_mode is None:
        h_ref, g_ref, wu_ref, wd_ref, out_ref = refs
        h = h_ref[...]
    else:
        h_ref, o_ref, wo_ref, g_ref, wu_ref, wd_ref, out_ref = refs
        o = o_ref[...]
        if o_mode == "transposed":
            proj = lax.dot_general(o, wo_ref[...], (((0,), (0,)), ((), ())), preferred_element_type=F32)
        else:
            proj = _dot(o, wo_ref[...])
        h = h_ref[...] + proj
    out_ref[...] = _mlp_body(h, g_ref[...], wu_ref, wd_ref, n_chunks)


def _mlp(h, g, w_up, w_down, layer, o=None, w_o=None, o_layer=None, o_transposed=False):
    N, D = h.shape
    F = w_up.shape[2]
    tm = TOK_BLOCK
    tok = pl.BlockSpec((tm, D), lambda i: (i, 0))
    in_specs = [tok]
    args = [h]
    o_mode = None
    if o is not None:
        if o_transposed:
            o_mode = "transposed"
            nb = o.shape[2] // tm
            in_specs.append(pl.BlockSpec((None, D, tm), lambda i: (i // nb, 0, i % nb)))
        else:
            o_mode = "rows"
            in_specs.append(tok)
        in_specs.append(_layer_spec((D, D), o_layer))
        args += [o, w_o]
    in_specs += [_const_spec((1, D)), _layer_spec((D, F), layer), _layer_spec((F, D), layer)]
    args += [g, w_up, w_down]
    return pl.pallas_call(
        functools.partial(_mlp_kernel, o_mode=o_mode, n_chunks=F // FF_CHUNK),
        grid=(N // tm,),
        in_specs=in_specs,
        out_specs=tok,
        out_shape=jax.ShapeDtypeStruct((N, D), F32),
        compiler_params=_params("arbitrary"),
        name="mlp" if o is None else "mlp_o",
    )(*args)


def _log_sigmoid(z):
    return -(jnp.maximum(-z, 0.0) + jnp.log1p(jnp.exp(-jnp.abs(z))))


def _gates(hn, wf_ref, bf_ref):
    return _log_sigmoid(_dot(hn, wf_ref[...]) + bf_ref[...])


def _kv_sample_kernel(h_ref, g_ref, wk_ref, wv_ref, gk_ref, wf_ref, bf_ref, e_ref, et_ref,
                      k_ref, v_ref, lf_ref, kb_ref, vb_ref, *, head_dim):
    hn = _rms(h_ref[...], g_ref[...]).astype(BF16)
    k = _proj_head_rms(hn, wk_ref, gk_ref, e_ref, et_ref, head_dim)
    v = _dot(hn, wv_ref[...])
    lf = _gates(hn, wf_ref, bf_ref)
    k_ref[...] = k
    v_ref[...] = v
    kb_ref[...] = k.astype(BF16)
    vb_ref[...] = v.astype(BF16)
    lf_ref[...] = lf[:, :N_HEADS]


def _kv_sample(h, g, wk, wv, gk_t, wf_p, bf_p, e, et):
    N, D = h.shape
    tm = TOK_BLOCK
    tok = pl.BlockSpec((tm, D), lambda i: (i, 0))
    return pl.pallas_call(
        functools.partial(_kv_sample_kernel, head_dim=D // N_HEADS),
        grid=(N // tm,),
        in_specs=[tok, _const_spec((1, D)), _const_spec((D, D)), _const_spec((D, D)), _const_spec((1, D)),
                  _const_spec((D, LANES)), _const_spec((1, LANES)), _const_spec((D, LANES)),
                  _const_spec((LANES, D))],
        out_specs=[tok, tok, pl.BlockSpec((tm, N_HEADS), lambda i: (i, 0)), tok, tok],
        out_shape=[jax.ShapeDtypeStruct((N, D), F32), jax.ShapeDtypeStruct((N, D), F32),
                   jax.ShapeDtypeStruct((N, N_HEADS), F32),
                   jax.ShapeDtypeStruct((N, D), BF16), jax.ShapeDtypeStruct((N, D), BF16)],
        compiler_params=_params("arbitrary"),
        name="shared_kv_sample",
    )(h, g, wk, wv, gk_t, wf_p, bf_p, e, et)


def _kv_prompt_kernel(h_ref, g_ref, wkt_ref, wvt_ref, gk_ref, wf_ref, bf_ref, tril_ref, spread_ref,
                      kt_ref, vt_ref, lft_ref, ct_ref, ka_ref, va_ref, carry_ref, *, nb, head_dim):
    i = pl.program_id(0)
    hn = _rms(h_ref[...], g_ref[...]).astype(BF16)
    tm = hn.shape[0]
    y = _dot_nt(wkt_ref[...], hn).reshape(N_HEADS, head_dim, tm)
    v_t = _dot_nt(wvt_ref[...], hn)
    lf = _gates(hn, wf_ref, bf_ref)
    k_t = (y * lax.rsqrt(jnp.mean(y * y, axis=1, keepdims=True) + EPS) * gk_ref[...]).reshape(N_HEADS * head_dim, tm)
    k = k_t.T
    lane = lax.broadcasted_iota(jnp.int32, (tm, LANES), 1)
    lf = jnp.where(lane < N_HEADS, lf, 0.0)
    kt_ref[...] = k_t
    vt_ref[...] = v_t
    lft_ref[...] = lf.T[:N_HEADS]

    @pl.when(i % nb == 0)
    def _():
        carry_ref[...] = jnp.zeros_like(carry_ref)

    hi, mid, lo = (t.astype(F32) for t in _split3(lf))
    sums = _dot(tril_ref[...], (hi + pltpu.roll(mid, N_HEADS, 1) + pltpu.roll(lo, 2 * N_HEADS, 1)).astype(BF16))
    sums = sums + pltpu.roll(sums, LANES - N_HEADS, 1) + pltpu.roll(sums, LANES - 2 * N_HEADS, 1)
    c = jnp.where(lane < N_HEADS, sums, 0.0) + carry_ref[...]
    carry_ref[...] = c[tm - 1:, :]
    ct_ref[...] = c.T[:N_HEADS]

    hi, mid, lo = (t.astype(F32) for t in _split3(c * -LOG2E))
    terms = hi + pltpu.roll(mid, N_HEADS, 1) + pltpu.roll(lo, 2 * N_HEADS, 1)
    ones = jnp.where((lane >= head_dim + 3) & (lane < head_dim + 6), 1.0, 0.0)
    aug = _dot(terms.astype(BF16), spread_ref[...])
    low = lane < head_dim
    for p in range(k.shape[1] // LANES):
        kp = k[:, p * LANES:(p + 1) * LANES]
        even = jnp.where(low, kp, ones) + aug[:, (2 * p) * LANES:(2 * p + 1) * LANES]
        odd = jnp.where(low, pltpu.roll(kp, head_dim, 1), ones) + aug[:, (2 * p + 1) * LANES:(2 * p + 2) * LANES]
        ka_ref[:, (2 * p) * LANES:(2 * p + 1) * LANES] = even.astype(BF16)
        ka_ref[:, (2 * p + 1) * LANES:(2 * p + 2) * LANES] = odd.astype(BF16)

    va_ref[:, 0:head_dim, :] = v_t.reshape(N_HEADS, head_dim, tm).astype(BF16)
    sub = lax.broadcasted_iota(jnp.int32, (N_HEADS, V_EXTRA, tm), 1)
    va_ref[:, head_dim:, :] = jnp.where(sub == 0, 1.0, 0.0).astype(BF16)


def _kv_prompt(h, T, g, wk_t, wv_t, gk_col, wf_p, bf_p):
    N, D = h.shape
    B = N // T
    head_dim = D // N_HEADS
    tm = TOK_BLOCK
    nb = T // tm
    tok = pl.BlockSpec((tm, D), lambda i: (i, 0))
    tril = (jnp.arange(tm)[:, None] >= jnp.arange(tm)[None, :]).astype(BF16)
    src = jnp.arange(LANES)[:, None]
    dst = jnp.arange(N_HEADS * LANES)[None, :]
    spread = ((src < 3 * N_HEADS) & (dst // LANES == src % N_HEADS)
              & (dst % LANES == head_dim + src // N_HEADS)).astype(BF16)
    seq_t = lambda r: pl.BlockSpec((None, r, tm), lambda i: (i // nb, 0, i % nb))
    return pl.pallas_call(
        functools.partial(_kv_prompt_kernel, nb=nb, head_dim=head_dim),
        grid=(N // tm,),
        in_specs=[tok, _const_spec((1, D)), _const_spec((D, D)), _const_spec((D, D)),
                  _const_spec((N_HEADS, head_dim, 1)), _const_spec((D, LANES)), _const_spec((1, LANES)),
                  _const_spec((tm, tm)), _const_spec((LANES, N_HEADS * LANES))],
        out_specs=[seq_t(D), seq_t(D), seq_t(N_HEADS), seq_t(N_HEADS),
                   pl.BlockSpec((tm, N_HEADS * LANES), lambda i: (i, 0)),
                   pl.BlockSpec((None, N_HEADS, head_dim + V_EXTRA, tm), lambda i: (i // nb, 0, 0, i % nb))],
        out_shape=[jax.ShapeDtypeStruct((B, D, T), F32), jax.ShapeDtypeStruct((B, D, T), F32),
                   jax.ShapeDtypeStruct((B, N_HEADS, T), F32), jax.ShapeDtypeStruct((B, N_HEADS, T), F32),
                   jax.ShapeDtypeStruct((N, N_HEADS * LANES), BF16),
                   jax.ShapeDtypeStruct((B, N_HEADS, head_dim + V_EXTRA, T), BF16)],
        scratch_shapes=[pltpu.VMEM((1, LANES), F32)],
        compiler_params=_params("arbitrary"),
        name="shared_kv_prompt",
    )(h, g, wk_t, wv_t, gk_col, wf_p, bf_p, tril, spread)


def _q_kernel(h_ref, g_ref, wq_ref, gq_ref, e_ref, et_ref, q_ref, *, head_dim):
    xn = _rms(h_ref[...], g_ref[...]).astype(BF16)
    q = _proj_head_rms(xn, wq_ref, gq_ref, e_ref, et_ref, head_dim)
    q_ref[...] = (q * (head_dim ** -0.5)).astype(BF16)


def _qproj(h, g, wq, layer, gq_t, e, et):
    N, D = h.shape
    tm = TOK_BLOCK
    tok = pl.BlockSpec((tm, D), lambda i: (i, 0))
    return pl.pallas_call(
        functools.partial(_q_kernel, head_dim=D // N_HEADS),
        grid=(N // tm,),
        in_specs=[tok, _const_spec((1, D)), _layer_spec((D, D), layer), _const_spec((1, D)),
                  _const_spec((D, LANES)), _const_spec((LANES, D))],
        out_specs=tok,
        out_shape=jax.ShapeDtypeStruct((N, D), BF16),
        compiler_params=_params("arbitrary"),
        name="q_proj",
    )(h, g, wq, gq_t, e, et)


def _q_t_kernel(h_ref, g_ref, wqt_ref, gq_ref, ct_ref, bound_ref, q_ref, *, head_dim):
    xn = _rms(h_ref[...], g_ref[...]).astype(BF16)
    tm = xn.shape[0]
    y = _dot_nt(wqt_ref[...], xn).reshape(N_HEADS, head_dim, tm)
    r = lax.rsqrt(jnp.mean(y * y, axis=1, keepdims=True) + EPS)
    q_ref[:, 0:head_dim, :] = (y * r * gq_ref[...]).astype(BF16)
    shape = (N_HEADS, LANES - head_dim, tm)
    sub = lax.broadcasted_iota(jnp.int32, shape, 1)
    hi, mid, lo = (jnp.broadcast_to(t.astype(F32)[:, None, :], shape)
                   for t in _split3(ct_ref[...] * LOG2E - bound_ref[...]))
    tail = jnp.where(sub < 3, 1.0, jnp.where(sub == 3, hi, jnp.where(sub == 4, mid, jnp.where(sub == 5, lo, 0.0))))
    q_ref[:, head_dim:, :] = tail.astype(BF16)


def _qproj_t(h, g, wq_t, layer, gq_col, ct, bound):
    N, D = h.shape
    tm = TOK_BLOCK
    T = ct.shape[2]
    nb = T // tm
    head_dim = D // N_HEADS
    return pl.pallas_call(
        functools.partial(_q_t_kernel, head_dim=head_dim),
        grid=(N // tm,),
        in_specs=[pl.BlockSpec((tm, D), lambda i: (i, 0)), _const_spec((1, D)), _layer_spec((D, D), layer),
                  _const_spec((N_HEADS, head_dim, 1)),
                  pl.BlockSpec((None, N_HEADS, tm), lambda i: (i // nb, 0, i % nb)), _const_spec((1, 1))],
        out_specs=pl.BlockSpec((None, N_HEADS, LANES, tm), lambda i: (i // nb, 0, 0, i % nb)),
        out_shape=jax.ShapeDtypeStruct((N // T, N_HEADS, LANES, T), BF16),
        compiler_params=_params("arbitrary"),
        name="q_proj_t",
    )(h, g, wq_t, gq_col, ct, bound)


def _attn_prompt_kernel(q_ref, k_ref, v_ref, o_ref, *, blk, head_dim, bounded):
    qi = pl.program_id(2)
    n_h = q_ref.shape[0]
    q_t = [q_ref[x] for x in range(n_h)]
    key_iota = lax.broadcasted_iota(jnp.int32, (blk, blk), 0)
    qry_iota = lax.broadcasted_iota(jnp.int32, (blk, blk), 1)

    def step(j0, n_blocks, carry, diagonal_last):
        offs = [pl.multiple_of((j0 + b) * blk, blk) for b in range(n_blocks)]
        s_all = [[_dot(k_ref[pl.ds(off, blk), x * LANES:(x + 1) * LANES], q_t[x]) for x in range(n_h)]
                 for off in offs]
        out = []
        for x, (m, acc) in enumerate(carry):
            for b, off in enumerate(offs):
                s = s_all[b][x]
                if diagonal_last and b == n_blocks - 1:
                    s = jnp.where(key_iota <= qry_iota, s, NEG_BIG)
                if bounded:
                    acc = acc + _dot(v_ref[x, :, pl.ds(off, blk)], jnp.exp2(s).astype(BF16))
                    continue
                m_new = jnp.maximum(m, jnp.max(s, axis=0, keepdims=True))
                p = jnp.exp2(s - m_new).astype(BF16)
                acc = jnp.exp2(m - m_new) * acc + _dot(v_ref[x, :, pl.ds(off, blk)], p)
                m = m_new
            out.append((m, acc))
        return tuple(out)

    init = tuple((jnp.full((1, blk), NEG_BIG, F32), jnp.zeros((v_ref.shape[1], blk), F32)) for _ in range(n_h))
    n_wide = qi // ATT_UNROLL
    carry = lax.fori_loop(0, n_wide, lambda jj, c: step(jj * ATT_UNROLL, ATT_UNROLL, c, False), init)
    j0 = n_wide * ATT_UNROLL
    tails = [functools.partial(step, j0, r + 1, diagonal_last=True) for r in range(ATT_UNROLL)]
    for x, (_, acc) in enumerate(lax.switch(qi - j0, tails, carry)):
        o_ref[x] = (acc[:head_dim] / acc[head_dim:head_dim + 1]).astype(o_ref.dtype)


def _attn_prompt(q_t, k_aug, v_aug, bounded):
    B, H, _, T = q_t.shape
    vr = v_aug.shape[2]
    head_dim = vr - V_EXTRA
    blk = ATT_BLOCK
    hb = ATT_HEADS
    return pl.pallas_call(
        functools.partial(_attn_prompt_kernel, blk=blk, head_dim=head_dim, bounded=bounded),
        grid=(B, H // hb, T // blk),
        in_specs=[
            pl.BlockSpec((None, hb, LANES, blk), lambda b, h, i: (b, h, 0, i)),
            pl.BlockSpec((None, T, hb * LANES), lambda b, h, i: (b, 0, h)),
            pl.BlockSpec((None, hb, vr, T), lambda b, h, i: (b, h, 0, 0)),
        ],
        out_specs=pl.BlockSpec((None, hb, head_dim, blk), lambda b, h, i: (b, h, 0, i)),
        out_shape=jax.ShapeDtypeStruct((B, H, head_dim, T), BF16),
        compiler_params=_params("arbitrary", "arbitrary", "arbitrary"),
        name="attn_prompt_bounded" if bounded else "attn_prompt",
    )(q_t, k_aug, v_aug)


def _attn_sample_kernel(pt_ref, q_ref, lfn_ref, kn_ref, vn_ref, *refs, n_pages, page, ts, head_dim, new_pad):
    del pt_ref
    lf_refs = refs[:n_pages]
    k_refs = refs[n_pages:2 * n_pages]
    v_refs = refs[2 * n_pages:3 * n_pages]
    o_ref = refs[3 * n_pages]
    D = q_ref.shape[1]
    n_pairs = D // LANES
    tp = 8
    lane = lax.broadcasted_iota(jnp.int32, (tp, LANES), 1)
    first = lane < head_dim
    tri = (lax.broadcasted_iota(jnp.int32, (page, page), 0)
           <= lax.broadcasted_iota(jnp.int32, (page, page), 1)).astype(BF16)

    q = q_ref[...].astype(F32)
    q8 = jnp.concatenate([q, jnp.zeros((tp - ts, D), F32)], axis=0)
    q_pairs = []
    for p in range(n_pairs):
        qp = q8[:, p * LANES:(p + 1) * LANES]
        q_pairs.append(jnp.concatenate([jnp.where(first, qp, 0.0), jnp.where(first, 0.0, qp)], axis=0).astype(BF16))

    def running_sum(lf_t, carry):
        parts = _dot(jnp.concatenate(_split3(lf_t), axis=0), tri)
        return parts[:N_HEADS] + parts[N_HEADS:2 * N_HEADS] + parts[2 * N_HEADS:] + carry

    def head_rows(c):
        n = c.shape[1]
        return jnp.broadcast_to(c[:, None, :], (N_HEADS, tp, n)).reshape(N_HEADS * tp, n)

    carry = jnp.zeros((N_HEADS, 1), F32)
    s_pages = []
    for j in range(n_pages):
        c = running_sum(lf_refs[j][...], carry)
        carry = c[:, page - 1:]
        s = [_dot(q_pairs[p], k_refs[j][p * LANES:(p + 1) * LANES, :].astype(BF16)) for p in range(n_pairs)]
        s_pages.append(jnp.concatenate(s, axis=0) - head_rows(c))
    c = running_sum(lfn_ref[...], carry)[:, :new_pad]
    s = [_dot_nt(q_pairs[p], kn_ref[:, p * LANES:(p + 1) * LANES]) for p in range(n_pairs)]
    t_row = lax.broadcasted_iota(jnp.int32, (N_HEADS * tp, new_pad), 0) % tp
    t_key = lax.broadcasted_iota(jnp.int32, (N_HEADS * tp, new_pad), 1)
    s_new = jnp.where(t_key <= t_row, jnp.concatenate(s, axis=0) - head_rows(c), NEG_BIG)

    m_el = s_pages[0]
    for s in s_pages[1:]:
        m_el = jnp.maximum(m_el, s)
    m = jnp.maximum(jnp.max(m_el, axis=1, keepdims=True), jnp.max(s_new, axis=1, keepdims=True))
    l_el = jnp.zeros((N_HEADS * tp, page), F32)
    acc_t = jnp.zeros((D, N_HEADS * tp), F32)
    for j in range(n_pages):
        pj = jnp.exp(s_pages[j] - m)
        l_el = l_el + pj
        acc_t = acc_t + _dot(v_refs[j][...].astype(BF16), pj.T.astype(BF16))
    p_new = jnp.exp(s_new - m)
    l = jnp.sum(l_el, axis=1, keepdims=True) + jnp.sum(p_new, axis=1, keepdims=True)
    o = (acc_t.T + _dot(p_new.astype(BF16), vn_ref[...])) / l
    row_head = lax.broadcasted_iota(jnp.int32, (N_HEADS * tp, D), 0) // tp
    col_head = lax.broadcasted_iota(jnp.int32, (N_HEADS * tp, D), 1) // head_dim
    own = jnp.where(row_head == col_head, o, 0.0).reshape(N_HEADS, tp, D)
    o_ref[...] = jnp.sum(own, axis=0)[:ts].astype(o_ref.dtype)


def _attn_sample(q, lf_new_t, k_new, v_new, cache_k, cache_v, cache_logf, page_table):
    bs, ts, D = q.shape
    n_pages = page_table.shape[1]
    n_pool, page = cache_k.shape[0], cache_k.shape[1]
    assert page == LANES
    new_pad = k_new.shape[1]
    ck = jnp.transpose(cache_k, (0, 2, 3, 1)).reshape(n_pool, D, page)
    cv = jnp.transpose(cache_v, (0, 2, 3, 1)).reshape(n_pool, D, page)
    cl = jnp.transpose(cache_logf, (0, 2, 1))
    seq = lambda n: pl.BlockSpec((None, n, D), lambda b, pt: (b, 0, 0))
    paged = lambda r: [pl.BlockSpec((None, r, page), lambda b, pt, j=j: (pt[b, j], 0, 0)) for j in range(n_pages)]
    return pl.pallas_call(
        functools.partial(_attn_sample_kernel, n_pages=n_pages, page=page, ts=ts,
                          head_dim=D // N_HEADS, new_pad=new_pad),
        grid_spec=pltpu.PrefetchScalarGridSpec(
            num_scalar_prefetch=1,
            grid=(bs,),
            in_specs=[seq(ts), pl.BlockSpec((None, N_HEADS, page), lambda b, pt: (b, 0, 0)),
                      seq(new_pad), seq(new_pad)] + paged(N_HEADS) + paged(D) + paged(D),
            out_specs=seq(ts),
        ),
        out_shape=jax.ShapeDtypeStruct((bs, ts, D), BF16),
        compiler_params=_params("arbitrary"),
        name="attn_sample",
    )(page_table, q, lf_new_t, k_new, v_new, *([cl] * n_pages), *([ck] * n_pages), *([cv] * n_pages))


def kernel(x_prompt, x_sample, state_pool, cache_k, cache_v, cache_logf, page_table, g_pool, w_pool, pool_scale,
           g_attn, w_q, g_q, w_o, g_kv, w_k, w_v, g_k, w_f, b_f, g_mlp, w_up, w_down):
    Bp, Tp, D = x_prompt.shape
    Bs, Ts, _ = x_sample.shape
    n_a = g_pool.shape[0]
    n_b = g_attn.shape[0]
    head_dim = D // N_HEADS
    past = page_table.shape[1] * cache_k.shape[1]
    new_pad = 16

    row = lambda a: a.reshape(1, -1).astype(F32)
    w_pool_b, w_up_b, w_down_b, w_q_b, w_o_b = (w.astype(BF16) for w in (w_pool, w_up, w_down, w_q, w_o))
    w_k_b, w_v_b = w_k.astype(BF16), w_v.astype(BF16)
    w_qt_b = jnp.transpose(w_q_b, (0, 2, 1))
    w_f_p = jnp.pad(w_f, ((0, 0), (0, LANES - N_HEADS))).astype(BF16)
    b_f_p = jnp.pad(b_f.astype(F32), (0, LANES - N_HEADS)).reshape(1, LANES)
    e = (jnp.arange(D)[:, None] // head_dim == jnp.arange(LANES)[None, :]).astype(BF16)
    et = e.T
    g_k_t = row(jnp.tile(g_k, N_HEADS))

    h_p = x_prompt
    h_s = jnp.transpose(x_sample, (1, 0, 2))
    st_p, st_s = [], []
    for l in range(n_a):
        args = (row(g_pool[l]), w_pool_b, l, row(pool_scale[l]))
        h_p, tail = _pool_mlp_prompt(h_p, *args, row(g_mlp[l]), w_up_b, w_down_b, l)
        st_p.append(tail[:, HALO - POOL_BUF:])
        h_s, st = _pool_sample(h_s, jnp.transpose(state_pool[l], (1, 0, 2)), *args, past)
        st_s.append(jnp.transpose(st, (1, 0, 2)))
        h_s = _mlp(h_s.reshape(Ts * Bs, D), row(g_mlp[l]), w_up_b, w_down_b, l).reshape(Ts, Bs, D)
    pool_state_prompt = jnp.stack(st_p, axis=0)
    pool_state_sample = jnp.stack(st_s, axis=0)

    h_p = h_p.reshape(Bp * Tp, D)
    h_s = jnp.transpose(h_s, (1, 0, 2)).reshape(Bs * Ts, D)

    g_k_col = jnp.tile(g_k, N_HEADS).astype(F32).reshape(N_HEADS, head_dim, 1)
    kt_p, vt_p, lft_p, ct_p, ka_p, va_p = _kv_prompt(h_p, Tp, row(g_kv), w_k_b.T, w_v_b.T, g_k_col, w_f_p, b_f_p)
    k_s, v_s, lf_s, kb_s, vb_s = _kv_sample(h_s, row(g_kv), w_k_b, w_v_b, g_k_t, w_f_p, b_f_p, e, et)
    pad_new = lambda a, n: jnp.pad(a.reshape(Bs, Ts, -1), ((0, 0), (0, n - Ts), (0, 0)))
    lfn_t = jnp.transpose(pad_new(lf_s, cache_logf.shape[1]), (0, 2, 1))
    kn, vn = pad_new(kb_s, new_pad), pad_new(vb_s, new_pad)
    ka_p = ka_p.reshape(Bp, Tp, N_HEADS * LANES)

    for l in range(n_b):
        j = n_a + l
        g_q_t = row(jnp.tile(g_q[l], N_HEADS))
        bound = (1.02 * LOG2E * head_dim ** 0.5) * jnp.max(jnp.abs(g_q[l])) * jnp.max(jnp.abs(g_k))
        g_q_col = (jnp.tile(g_q[l], N_HEADS) * (LOG2E * head_dim ** -0.5)).astype(F32).reshape(N_HEADS, head_dim, 1)
        q_p = _qproj_t(h_p, row(g_attn[l]), w_qt_b, l, g_q_col, ct_p, bound.reshape(1, 1).astype(F32))
        o_p = lax.cond(2.0 * bound < MAX_SELF_DEFICIT,
                       functools.partial(_attn_prompt, bounded=True),
                       functools.partial(_attn_prompt, bounded=False), q_p, ka_p, va_p).reshape(Bp, D, Tp)
        h_p = _mlp(h_p, row(g_mlp[j]), w_up_b, w_down_b, j, o_p, w_o_b, l, o_transposed=True)
        q_s = _qproj(h_s, row(g_attn[l]), w_q_b, l, g_q_t, e, et)
        o_s = _attn_sample(q_s.reshape(Bs, Ts, D), lfn_t, kn, vn, cache_k, cache_v, cache_logf, page_table)
        h_s = _mlp(h_s, row(g_mlp[j]), w_up_b, w_down_b, j, o_s.reshape(Bs * Ts, D), w_o_b, l)

    hd_t = lambda a: jnp.transpose(a.reshape(Bp, N_HEADS, head_dim, Tp), (0, 3, 1, 2))
    hd4 = lambda a: a.reshape(Bs, Ts, N_HEADS, head_dim)
    return (h_p.reshape(Bp, Tp, D), h_s.reshape(Bs, Ts, D), pool_state_prompt, pool_state_sample,
            hd_t(kt_p), hd_t(vt_p), jnp.transpose(lft_p, (0, 2, 1)).astype(cache_logf.dtype),
            hd4(k_s), hd4(v_s), lf_s.reshape(Bs, Ts, N_HEADS).astype(cache_logf.dtype))
```
